```python
import jax
import jax.numpy as jnp
from jax import lax
import numpy as np

D_MODEL = 4096
BATCH = 4
SEQ = 2048
DEPTH = 4
DEC_BATCH = 8
DEC_SEQ = 4
PAST_LEN = 8192
PAGE_SIZE = 128

NSA_HEADS = 16
NSA_KV_HEADS = 4
NSA_GROUP = NSA_HEADS // NSA_KV_HEADS
HEAD_DIM = 128
CMP_BLOCK = 32
CMP_STRIDE = 16
SEL_BLOCK = 64
SEL_TOP_N = 16
WINDOW = 512
ROPE_THETA = 10000.0
QUERY_CHUNK = 16
BAND_BLOCK = 128
FORCE_SCORE = 1e4
NEG_INF = -1e30
POOL_DIM = D_MODEL // 2
POOL_WINDOWS = (2, 4, 8, 16)
POOL_GROUPS = len(POOL_WINDOWS)
POOL_GROUP_DIM = POOL_DIM // POOL_GROUPS
POOL_BUF = max(POOL_WINDOWS) - 1
GLA_HEADS = 4
GLA_DK = D_MODEL // (2 * GLA_HEADS)
GLA_DV = D_MODEL // GLA_HEADS
GLA_GATE_RANK = 16
GLA_GATE_NORM = 16.0
GLA_CHUNK = 32
N_EXPERTS = 32
TOP_K = 4
D_FF = D_MODEL // 4
SWIGLU_LIMIT = 7.0
SWIGLU_ALPHA = 1.702
MOE_MAX_BLOCK = 256
LN_EPS = 1e-5
RMS_EPS = 1e-6
DN_ALPHA = (2 * DEPTH) ** 0.25
DN_BETA = (8 * DEPTH) ** -0.25
N_EVEN = (DEPTH + 1) // 2
N_ODD = DEPTH // 2
NSA_Q_DIM = NSA_HEADS * HEAD_DIM
NSA_KV_DIM = NSA_KV_HEADS * HEAD_DIM
EVEN_SPLITS = (NSA_Q_DIM, 6 * NSA_KV_DIM, 3 * NSA_HEADS, POOL_DIM)
EVEN_IN = sum(EVEN_SPLITS)
EVEN_MIX = NSA_Q_DIM + POOL_DIM
GLA_SPLITS = (GLA_HEADS * GLA_DK, GLA_HEADS * GLA_DK, GLA_HEADS * GLA_DV, GLA_HEADS * GLA_DV, GLA_GATE_RANK)
ODD_IN = sum(GLA_SPLITS)
ODD_MIX = GLA_HEADS * GLA_DV

kernel_name = 'nsa_pool_gla_moe_deepnorm_step'


def _split(x, sizes):
    return jnp.split(x, [int(i) for i in np.cumsum(sizes)[:-1]], axis=-1)


def _layer_norm(x, g, b):
    xf = x.astype(jnp.float32)
    mu = jnp.mean(xf, -1, keepdims=True)
    var = jnp.mean(jnp.square(xf - mu), -1, keepdims=True)
    return ((xf - mu) * lax.rsqrt(var + LN_EPS) * g + b).astype(x.dtype)


def _rope(x, pos):
    half = HEAD_DIM // 2
    inv = ROPE_THETA ** (-2.0 * jnp.arange(half, dtype=jnp.float32) / HEAD_DIM)
    ang = pos.astype(jnp.float32)[:, None] * inv[None, :]
    cos = jnp.cos(ang)[:, None, :]
    sin = jnp.sin(ang)[:, None, :]
    xf = x.astype(jnp.float32)
    x1, x2 = xf[..., :half], xf[..., half:]
    return jnp.concatenate([x1 * cos - x2 * sin, x2 * cos + x1 * sin], axis=-1).astype(x.dtype)


def _attend(q, k, v, mask):
    s = jnp.einsum('btgjd,bsgd->btgjs', q, k).astype(jnp.float32)
    s = jnp.where(mask[:, :, None, None, :], s, NEG_INF)
    p = jax.nn.softmax(s, axis=-1)
    return jnp.einsum('btgjs,bsgd->btgjd', p.astype(v.dtype), v)


def _compress(rows, w1, w2):
    B, T = rows.shape[:2]
    n_ch = -(-T // CMP_STRIDE)
    rows = jnp.pad(rows, ((0, 0), (0, n_ch * CMP_STRIDE - T), (0, 0), (0, 0), (0, 0)))
    ch = rows.reshape(B, n_ch, CMP_STRIDE, 2, NSA_KV_HEADS, HEAD_DIM)
    r = CMP_BLOCK // CMP_STRIDE
    n_blk = n_ch - r + 1
    blocks = jnp.concatenate([ch[:, i:i + n_blk] for i in range(r)], axis=2)
    hid = jax.nn.gelu(jnp.einsum('bnjcgd,cjde->bncge', blocks, w1))
    return jnp.einsum('bncge,cef->bncgf', hid, w2)


def _cmp_to_sel(n_cmp, n_sel):
    r = CMP_BLOCK // CMP_STRIDE
    per = SEL_BLOCK // CMP_STRIDE
    n = jnp.arange(n_cmp)[:, None]
    s = jnp.arange(n_sel)[None, :]
    return sum(((n + i) // per == s).astype(jnp.float32) for i in range(r))


def _cmp_sel_branch(q, pos, ck, cv, n_sel, gather_sel):
    n_cmp = ck.shape[1]
    s = jnp.einsum('btgjd,bngd->btgjn', q, ck).astype(jnp.float32)
    vis = (jnp.arange(n_cmp) * CMP_STRIDE + CMP_BLOCK - 1)[None, :] <= pos[:, None]
    vis = vis[None, :, None, None, :]
    p = jax.nn.softmax(jnp.where(vis, s, NEG_INF), axis=-1) * vis
    o_cmp = jnp.einsum('btgjn,bngd->btgjd', p.astype(cv.dtype), cv)
    imp = jnp.einsum('btgjn,ns->btgs', p, _cmp_to_sel(n_cmp, n_sel))
    blk = jnp.arange(n_sel)[None, :]
    cur = (pos // SEL_BLOCK)[:, None]
    forced = ((blk == 0) | (blk == cur) | (blk == cur - 1))[None, :, None, :]
    valid = (blk * SEL_BLOCK <= pos[:, None])[None, :, None, :]
    score = jnp.where(forced, FORCE_SCORE, jnp.where(valid, imp, -1.0))
    _, idx = lax.top_k(score, min(SEL_TOP_N, n_sel))
    k_sel, v_sel = gather_sel(idx)
    kpos = idx[..., None] * SEL_BLOCK + jnp.arange(SEL_BLOCK)
    ok = (kpos <= pos[None, :, None, None, None])[:, :, :, None]
    s2 = jnp.einsum('btgjd,btgksd->btgjks', q, k_sel).astype(jnp.float32)
    s2 = jnp.where(ok, s2, NEG_INF)
    shp = s2.shape
    p2 = jax.nn.softmax(s2.reshape(shp[:4] + (-1,)), axis=-1).reshape(shp)
    o_sel = jnp.einsum('btgjks,btgksd->btgjd', p2.astype(v_sel.dtype), v_sel)
    return o_cmp, o_sel


def _gather_local(rows):
    B, T = rows.shape[:2]
    blocks = rows.reshape(B, T // SEL_BLOCK, SEL_BLOCK, 2, NSA_KV_HEADS, HEAD_DIM)
    b_i = jnp.arange(B)[:, None, None, None]
    g_i = jnp.arange(NSA_KV_HEADS)[None, None, :, None]

    def gather(idx):
        kv = blocks[b_i, idx, :, :, g_i]
        return kv[..., 0, :], kv[..., 1, :]
    return gather


def _gather_paged(pool, page_table, new_rows):
    bpp = PAGE_SIZE // SEL_BLOCK
    pool_b = pool.reshape(-1, SEL_BLOCK, 2, NSA_KV_HEADS, HEAD_DIM)
    n_past = page_table.shape[1] * bpp
    DB, DS = new_rows.shape[:2]
    n_new = -(-DS // SEL_BLOCK)
    new_b = jnp.pad(new_rows, ((0, 0), (0, n_new * SEL_BLOCK - DS), (0, 0), (0, 0), (0, 0)))
    new_b = new_b.reshape(DB, n_new, SEL_BLOCK, 2, NSA_KV_HEADS, HEAD_DIM)
    b_i = jnp.arange(DB)[:, None, None, None]
    g_i = jnp.arange(NSA_KV_HEADS)[None, None, :, None]

    def gather(idx):
        past = jnp.minimum(idx, n_past - 1)
        phys = page_table[b_i, past // bpp] * bpp + past % bpp
        kv_past = pool_b[phys, :, :, g_i]
        kv_new = new_b[b_i, jnp.clip(idx - n_past, 0, n_new - 1), :, :, g_i]
        kv = jnp.where((idx < n_past)[..., None, None, None], kv_past, kv_new.astype(kv_past.dtype))
        return kv[..., 0, :], kv[..., 1, :]
    return gather


def _window_prompt(q, kw, vw):
    B, S = q.shape[:2]
    nb = S // BAND_BLOCK
    nk = -(-WINDOW // BAND_BLOCK)
    kw_len = (nk + 1) * BAND_BLOCK

    def band(a):
        ap = jnp.pad(a, ((0, 0), (nk * BAND_BLOCK, 0), (0, 0), (0, 0)))
        ap = ap.reshape(B, nb + nk, BAND_BLOCK, NSA_KV_HEADS, HEAD_DIM)
        return jnp.concatenate([ap[:, i:i + nb] for i in range(nk + 1)], axis=2).reshape(B * nb, kw_len, NSA_KV_HEADS, HEAD_DIM)
    qpos = jnp.arange(S).reshape(nb, BAND_BLOCK)[:, :, None]
    kpos = ((jnp.arange(nb)[:, None] - nk) * BAND_BLOCK + jnp.arange(kw_len)[None, :])[:, None, :]
    mask = (kpos <= qpos) & (kpos > qpos - WINDOW) & (kpos >= 0)
    mask = jnp.broadcast_to(mask[None], (B, nb, BAND_BLOCK, kw_len)).reshape(B * nb, BAND_BLOCK, kw_len)
    o = _attend(q.reshape(B * nb, BAND_BLOCK, NSA_KV_HEADS, NSA_GROUP, HEAD_DIM), band(kw), band(vw), mask)
    return o.reshape(B, S, NSA_KV_HEADS, NSA_GROUP, HEAD_DIM)


def _window_sample(q, pos, buf, kv_new):
    DS = q.shape[1]
    wb = buf.shape[1]
    kv = jnp.concatenate([buf, kv_new.astype(buf.dtype)], axis=1)
    kpos = (pos[0] - wb + jnp.arange(wb + DS))[None, :]
    qp = pos[:, None]
    mask = (kpos <= qp) & (kpos > qp - WINDOW) & (kpos >= 0)
    o = _attend(q, kv[:, :, 0], kv[:, :, 1], mask[None])
    return o, kv[:, -wb:]


def _pool_mix(u, prev, pos0, w_pool, scale):
    B, T, _ = u.shape
    ext = jnp.concatenate([prev, u.astype(prev.dtype)], axis=1)
    cs = jnp.pad(jnp.cumsum(ext.astype(jnp.float32), axis=1), ((0, 0), (1, 0), (0, 0)))
    n_pos = pos0 + jnp.arange(T, dtype=jnp.int32) + 1
    means = []
    for gi, w in enumerate(POOL_WINDOWS):
        c0, c1 = gi * POOL_GROUP_DIM, (gi + 1) * POOL_GROUP_DIM
        hi = cs[:, POOL_BUF + 1:POOL_BUF + 1 + T, c0:c1]
        lo = cs[:, POOL_BUF + 1 - w:POOL_BUF + 1 - w + T, c0:c1]
        cnt = jnp.minimum(n_pos, w).astype(jnp.float32)[None, :, None]
        means.append((hi - lo) / cnt)
    d = (jnp.concatenate(means, axis=-1) - u.astype(jnp.float32)).astype(u.dtype)
    y = jnp.einsum('btgc,gce->btge', d.reshape(B, T, POOL_GROUPS, POOL_GROUP_DIM), w_pool)
    return y.reshape(B, T, POOL_DIM) * scale, ext[:, -POOL_BUF:]


def _even_project(x, pos, w_in):
    B, T, _ = x.shape
    q, kv, gates, u = _split(x @ w_in, EVEN_SPLITS)
    q = _rope(q.reshape(B, T, NSA_HEADS, HEAD_DIM), pos) * (HEAD_DIM ** -0.5)
    q = q.reshape(B, T, NSA_KV_HEADS, NSA_GROUP, HEAD_DIM)
    kv = kv.reshape(B, T, 3, 2, NSA_KV_HEADS, HEAD_DIM)
    k = _rope(kv[:, :, :, 0].reshape(B, T, 3 * NSA_KV_HEADS, HEAD_DIM), pos)
    kv = jnp.stack([k.reshape(B, T, 3, NSA_KV_HEADS, HEAD_DIM), kv[:, :, :, 1]], axis=3)
    gates = jax.nn.sigmoid(gates.astype(jnp.float32)).reshape(B, T, NSA_KV_HEADS, NSA_GROUP, 3)
    return q, kv, gates, u


def _even_merge(o_cmp, o_sel, o_win, gates, pool_y, w_out):
    B, T = o_cmp.shape[:2]
    o = (gates[..., 0:1] * o_cmp.astype(jnp.float32) + gates[..., 1:2] * o_sel.astype(jnp.float32)
         + gates[..., 2:3] * o_win.astype(jnp.float32)).astype(pool_y.dtype)
    return jnp.concatenate([o.reshape(B, T, NSA_Q_DIM), pool_y], axis=-1) @ w_out


def _even_prompt(x, w_in, w_c1, w_c2, w_pool, pool_scale, w_out):
    B, S, _ = x.shape
    pos = jnp.arange(S, dtype=jnp.int32)
    q, kv, gates, u = _even_project(x, pos, w_in)
    kv_cmp, kv_sel, kv_win = kv[:, :, 0], kv[:, :, 1], kv[:, :, 2]
    comp = _compress(kv_cmp, w_c1, w_c2)
    ck, cv = comp[:, :, 0], comp[:, :, 1]
    gather = _gather_local(kv_sel)
    n_sel = S // SEL_BLOCK
    n_q = S // QUERY_CHUNK
    q_chunks = jnp.moveaxis(q.reshape(B, n_q, QUERY_CHUNK, NSA_KV_HEADS, NSA_GROUP, HEAD_DIM), 1, 0)
    p_chunks = pos.reshape(n_q, QUERY_CHUNK)
    o_cmp, o_sel = lax.map(lambda a: _cmp_sel_branch(a[0], a[1], ck, cv, n_sel, gather), (q_chunks, p_chunks))
    o_cmp = jnp.moveaxis(o_cmp, 0, 1).reshape(B, S, NSA_KV_HEADS, NSA_GROUP, HEAD_DIM)
    o_sel = jnp.moveaxis(o_sel, 0, 1).reshape(B, S, NSA_KV_HEADS, NSA_GROUP, HEAD_DIM)
    o_win = _window_prompt(q, kv_win[:, :, 0], kv_win[:, :, 1])
    pool_prev = jnp.zeros((B, POOL_BUF, POOL_DIM), u.dtype)
    pool_y, pool_state = _pool_mix(u, pool_prev, 0, w_pool, pool_scale)
    y = _even_merge(o_cmp, o_sel, o_win, gates, pool_y, w_out)
    wb = min(WINDOW, PAST_LEN)
    win_state = jnp.pad(kv_win, ((0, 0), (max(0, wb - S), 0), (0, 0), (0, 0), (0, 0)))[:, -wb:]
    return y, kv_cmp, kv_sel, win_state, pool_state


def _even_sample(x, cache_cmp, cache_sel, win_buf, pool_buf, page_table, w_in, w_c1, w_c2, w_pool, pool_scale, w_out):
    DB, DS, _ = x.shape
    pos = PAST_LEN + jnp.arange(DS, dtype=jnp.int32)
    q, kv, gates, u = _even_project(x, pos, w_in)
    kv_cmp, kv_sel, kv_win = kv[:, :, 0], kv[:, :, 1], kv[:, :, 2]
    past_cmp = cache_cmp[page_table].reshape(DB, -1, 2, NSA_KV_HEADS, HEAD_DIM)
    comp = _compress(jnp.concatenate([past_cmp, kv_cmp.astype(past_cmp.dtype)], axis=1), w_c1, w_c2)
    n_sel = -(-(PAST_LEN + DS) // SEL_BLOCK)
    gather = _gather_paged(cache_sel, page_table, kv_sel)
    o_cmp, o_sel = _cmp_sel_branch(q, pos, comp[:, :, 0], comp[:, :, 1], n_sel, gather)
    o_win, win_state = _window_sample(q, pos, win_buf, kv_win)
    pool_y, pool_state = _pool_mix(u, pool_buf, PAST_LEN, w_pool, pool_scale)
    y = _even_merge(o_cmp, o_sel, o_win, gates, pool_y, w_out)
    return y, kv_cmp, kv_sel, win_state, pool_state


def _gla_scan(q, k, v, log_a, s0):
    B, T = q.shape[:2]
    n = -(-T // GLA_CHUNK)
    pad = n * GLA_CHUNK - T

    def chunks(a):
        a = jnp.pad(a, ((0, 0), (0, pad), (0, 0), (0, 0)))
        return jnp.moveaxis(a.reshape((B, n, GLA_CHUNK) + a.shape[2:]), 1, 0)
    tri = jnp.tril(jnp.ones((GLA_CHUNK, GLA_CHUNK), dtype=bool))

    def step(s, inp):
        qc, kc, vc, ac = inp
        b = jnp.cumsum(ac, axis=1)
        qt = qc * jnp.exp(b)
        kt = kc * jnp.exp(-b)
        att = jnp.where(tri, jnp.einsum('bthk,bshk->bhts', qt, kt), 0.0)
        o = jnp.einsum('bthk,bhkv->bthv', qt, s) + jnp.einsum('bhts,bshv->bthv', att, vc)
        b_end = b[:, -1]
        s = s * jnp.exp(b_end)[..., None] + jnp.einsum('bshk,bshv->bhkv', kc * jnp.exp(b_end[:, None] - b), vc)
        return s, o
    s_fin, o = lax.scan(step, s0, (chunks(q), chunks(k), chunks(v), chunks(log_a)))
    o = jnp.moveaxis(o, 0, 1).reshape(B, n * GLA_CHUNK, GLA_HEADS, GLA_DV)[:, :T]
    return o, s_fin


def _gla_layer(x, s0, w_in, w_gate2, b_gate, norm_g, w_out):
    B, T, _ = x.shape
    q, k, v, g, lr = _split(x @ w_in, GLA_SPLITS)
    z = (lr @ w_gate2 + b_gate).astype(jnp.float32)
    log_a = (jax.nn.log_sigmoid(z) / GLA_GATE_NORM).reshape(B, T, GLA_HEADS, GLA_DK)

    def heads(t, d):
        return t.astype(jnp.float32).reshape(B, T, GLA_HEADS, d)
    o, s = _gla_scan(heads(q, GLA_DK) * (GLA_DK ** -0.5), heads(k, GLA_DK), heads(v, GLA_DV), log_a, s0.astype(jnp.float32))
    o = o * lax.rsqrt(jnp.mean(jnp.square(o), -1, keepdims=True) + RMS_EPS) * norm_g
    o = (o * jax.nn.silu(heads(g, GLA_DV))).astype(x.dtype).reshape(B, T, ODD_MIX)
    return o @ w_out, s.astype(x.dtype)


def _moe(x, w_r, b_r, w_g, b_g, w_u, b_u, w_d, b_d):
    B, T, D = x.shape
    n_tok = B * T
    n_slot = n_tok * TOP_K
    xf = x.reshape(n_tok, D)
    logits = (xf @ w_r).astype(jnp.float32) + b_r.astype(jnp.float32)
    top_val, top_exp = lax.top_k(logits, TOP_K)
    gate = jax.nn.softmax(top_val, axis=-1)
    blk = 8
    while blk < MOE_MAX_BLOCK and 2 * blk * N_EXPERTS <= n_slot:
        blk *= 2
    flat_e = top_exp.reshape(-1).astype(jnp.int32)
    order = jnp.argsort(flat_e * n_slot + jnp.arange(n_slot, dtype=jnp.int32))
    e_sorted = flat_e[order]
    counts = jnp.zeros((N_EXPERTS,), jnp.int32).at[flat_e].add(1)
    padded = (counts + blk - 1) // blk * blk
    start = jnp.cumsum(counts) - counts
    pend = jnp.cumsum(padded)
    dest = (pend - padded)[e_sorted] + jnp.arange(n_slot, dtype=jnp.int32) - start[e_sorted]
    n_blk = -(-n_slot // blk) + N_EXPERTS
    cap = n_blk * blk
    slot_tok = jnp.full((cap,), n_tok, jnp.int32).at[dest].set((order // TOP_K).astype(jnp.int32))
    slot_gate = jnp.zeros((cap,), jnp.float32).at[dest].set(gate.reshape(-1)[order])
    blk_exp = jnp.minimum(jnp.searchsorted(pend, jnp.arange(n_blk, dtype=jnp.int32) * blk, side='right'), N_EXPERTS - 1)
    x_pad = jnp.concatenate([xf, jnp.zeros((1, D), xf.dtype)], axis=0)

    def expert_block(args):
        tok, g, e = args
        xb = x_pad[tok]
        hg = jnp.minimum(xb @ w_g[e] + b_g[e], SWIGLU_LIMIT)
        hu = jnp.clip(xb @ w_u[e] + b_u[e], -SWIGLU_LIMIT, SWIGLU_LIMIT)
        act = hg * jax.nn.sigmoid(SWIGLU_ALPHA * hg) * (hu + 1.0)
        return (act @ w_d[e] + b_d[e]) * g[:, None].astype(xf.dtype)
    y = lax.map(expert_block, (slot_tok.reshape(n_blk, blk), slot_gate.reshape(n_blk, blk), blk_exp))
    out = jnp.zeros((n_tok + 1, D), xf.dtype).at[slot_tok].add(y.reshape(cap, D).astype(xf.dtype))
    return out[:n_tok].reshape(B, T, D)


def setup_inputs(seed: int = 0) -> dict:
    key = jax.random.key(seed)
    ks = iter(jax.random.split(key, 32))

    def nrm(shape, scale=1.0):
        return jax.random.normal(next(ks), shape, jnp.float32) * scale
    n_pages = PAST_LEN // PAGE_SIZE
    n_pool = (5 * DEC_BATCH * n_pages + 3) // 4
    wb = min(WINDOW, PAST_LEN)
    kv_row = (2, NSA_KV_HEADS, HEAD_DIM)
    perm = jax.random.permutation(next(ks), n_pool)
    page_table = perm[:DEC_BATCH * n_pages].reshape(DEC_BATCH, n_pages).astype(jnp.int32)
    return {
        'x_prompt': nrm((BATCH, SEQ, D_MODEL)),
        'x_sample': nrm((DEC_BATCH, DEC_SEQ, D_MODEL)),
        'cache_nsa_cmp': nrm((N_EVEN, n_pool, PAGE_SIZE) + kv_row),
        'cache_nsa_sel': nrm((N_EVEN, n_pool, PAGE_SIZE) + kv_row),
        'state_nsa_win': nrm((N_EVEN, DEC_BATCH, wb) + kv_row),
        'state_pool': nrm((N_EVEN, DEC_BATCH, POOL_BUF, POOL_DIM)),
        'state_gla': nrm((N_ODD, DEC_BATCH, GLA_HEADS, GLA_DK, GLA_DV)),
        'page_table': page_table,
        'nsa_w_in': nrm((N_EVEN, D_MODEL, EVEN_IN), D_MODEL ** -0.5),
        'nsa_w_cmp1': nrm((N_EVEN, 2, CMP_BLOCK, HEAD_DIM, HEAD_DIM), (CMP_BLOCK * HEAD_DIM) ** -0.5),
        'nsa_w_cmp2': nrm((N_EVEN, 2, HEAD_DIM, HEAD_DIM), (2.0 / HEAD_DIM) ** 0.5),
        'pool_w': nrm((N_EVEN, POOL_GROUPS, POOL_GROUP_DIM, POOL_GROUP_DIM), POOL_GROUP_DIM ** -0.5),
        'pool_scale': 1.0 + nrm((N_EVEN, POOL_DIM), 0.1),
        'even_w_out': nrm((N_EVEN, EVEN_MIX, D_MODEL), EVEN_MIX ** -0.5 * DN_BETA),
        'gla_w_in': nrm((N_ODD, D_MODEL, ODD_IN), D_MODEL ** -0.5),
        'gla_w_gate2': nrm((N_ODD, GLA_GATE_RANK, GLA_HEADS * GLA_DK), GLA_GATE_RANK ** -0.5),
        'gla_b_gate': nrm((N_ODD, GLA_HEADS * GLA_DK), 0.1),
        'gla_norm_g': 1.0 + nrm((N_ODD, GLA_DV), 0.02),
        'odd_w_out': nrm((N_ODD, ODD_MIX, D_MODEL), ODD_MIX ** -0.5 * DN_BETA),
        'moe_w_router': nrm((DEPTH, D_MODEL, N_EXPERTS), D_MODEL ** -0.5),
        'moe_b_router': nrm((DEPTH, N_EXPERTS), 0.01),
        'moe_w_gate': nrm((DEPTH, N_EXPERTS, D_MODEL, D_FF), D_MODEL ** -0.5),
        'moe_b_gate': nrm((DEPTH, N_EXPERTS, D_FF), 0.01),
        'moe_w_up': nrm((DEPTH, N_EXPERTS, D_MODEL, D_FF), D_MODEL ** -0.5),
        'moe_b_up': nrm((DEPTH, N_EXPERTS, D_FF), 0.01),
        'moe_w_down': nrm((DEPTH, N_EXPERTS, D_FF, D_MODEL), D_FF ** -0.5 * DN_BETA),
        'moe_b_down': nrm((DEPTH, N_EXPERTS, D_MODEL), 0.01),
        'ln_g': 1.0 + nrm((DEPTH, 2, D_MODEL), 0.02),
        'ln_b': nrm((DEPTH, 2, D_MODEL), 0.02),
    }


def reference(x_prompt, x_sample, cache_nsa_cmp, cache_nsa_sel, state_nsa_win, state_pool, state_gla, page_table,
              nsa_w_in, nsa_w_cmp1, nsa_w_cmp2, pool_w, pool_scale, even_w_out,
              gla_w_in, gla_w_gate2, gla_b_gate, gla_norm_g, odd_w_out,
              moe_w_router, moe_b_router, moe_w_gate, moe_b_gate, moe_w_up, moe_b_up, moe_w_down, moe_b_down,
              ln_g, ln_b):
    xp, xs = x_prompt, x_sample
    cmp_p, cmp_s, sel_p, sel_s, win_p, win_s, pool_p, pool_s, gla_p, gla_s = ([] for _ in range(10))
    for layer in range(DEPTH):
        i = layer // 2
        if layer % 2 == 0:
            ew = (nsa_w_in[i], nsa_w_cmp1[i], nsa_w_cmp2[i], pool_w[i], pool_scale[i], even_w_out[i])
            mp, c_p, s_p, w_p, q_p = _even_prompt(xp, *ew)
            ms, c_s, s_s, w_s, q_s = _even_sample(xs, cache_nsa_cmp[i], cache_nsa_sel[i], state_nsa_win[i],
                                                  state_pool[i], page_table, *ew)
            cmp_p.append(c_p)
            cmp_s.append(c_s)
            sel_p.append(s_p)
            sel_s.append(s_s)
            win_p.append(w_p)
            win_s.append(w_s)
            pool_p.append(q_p)
            pool_s.append(q_s)
        else:
            ow = (gla_w_in[i], gla_w_gate2[i], gla_b_gate[i], gla_norm_g[i], odd_w_out[i])
            s0 = jnp.zeros((xp.shape[0], GLA_HEADS, GLA_DK, GLA_DV), xp.dtype)
            mp, g_p = _gla_layer(xp, s0, *ow)
            ms, g_s = _gla_layer(xs, state_gla[i], *ow)
            gla_p.append(g_p)
            gla_s.append(g_s)
        xp = _layer_norm(DN_ALPHA * xp + mp, ln_g[layer, 0], ln_b[layer, 0])
        xs = _layer_norm(DN_ALPHA * xs + ms, ln_g[layer, 0], ln_b[layer, 0])
        mw = (moe_w_router[layer], moe_b_router[layer], moe_w_gate[layer], moe_b_gate[layer],
              moe_w_up[layer], moe_b_up[layer], moe_w_down[layer], moe_b_down[layer])
        xp = _layer_norm(DN_ALPHA * xp + _moe(xp, *mw), ln_g[layer, 1], ln_b[layer, 1])
        xs = _layer_norm(DN_ALPHA * xs + _moe(xs, *mw), ln_g[layer, 1], ln_b[layer, 1])
    return (xp, xs, jnp.stack(cmp_p), jnp.stack(cmp_s), jnp.stack(sel_p), jnp.stack(sel_s),
            jnp.stack(win_p), jnp.stack(win_s), jnp.stack(pool_p), jnp.stack(pool_s),
            jnp.stack(gla_p), jnp.stack(gla_s))
```

```python
import functools

import jax
import jax.numpy as jnp
import numpy as np
from jax import lax
from jax.experimental import pallas as pl
from jax.experimental.pallas import tpu as pltpu

F32 = jnp.float32
BF16 = jnp.bfloat16

D_MODEL = 4096
DEPTH = 4
PAGE_SIZE = 128
NSA_HEADS = 16
NSA_KV_HEADS = 4
NSA_GROUP = NSA_HEADS // NSA_KV_HEADS
HEAD_DIM = 128
CMP_BLOCK = 32
CMP_STRIDE = 16
SEL_BLOCK = 64
SEL_TOP_N = 16
WINDOW = 512
ROPE_THETA = 10000.0
FORCE_SCORE = 1e4
NEG_INF = -1e30
POOL_DIM = D_MODEL // 2
POOL_WINDOWS = (2, 4, 8, 16)
POOL_GROUP_DIM = POOL_DIM // len(POOL_WINDOWS)
POOL_BUF = max(POOL_WINDOWS) - 1
POOL_HALO = 16
GLA_HEADS = 4
GLA_DK = D_MODEL // (2 * GLA_HEADS)
GLA_DV = D_MODEL // GLA_HEADS
GLA_GATE_RANK = 16
GLA_GATE_NORM = 16.0
GLA_CHUNK = 32
N_EXPERTS = 32
TOP_K = 4
D_FF = D_MODEL // 4
SWIGLU_LIMIT = 7.0
SWIGLU_ALPHA = 1.702
LN_EPS = 1e-5
RMS_EPS = 1e-6
DN_ALPHA = (2 * DEPTH) ** 0.25
NSA_Q_DIM = NSA_HEADS * HEAD_DIM
NSA_KV_DIM = NSA_KV_HEADS * HEAD_DIM
EVEN_MAIN = NSA_Q_DIM + 6 * NSA_KV_DIM
EVEN_GATES = 3 * NSA_HEADS
ODD_MAIN = 2 * GLA_HEADS * GLA_DK + 2 * GLA_HEADS * GLA_DV

LANE = 128
VMEM_LIMIT = 56 * 1024 * 1024
TM_DENSE = 1408
TN_DENSE = 512
TM_ROW = 256
TM_MOE = 384
TF_MOE = 512
TN_MOE = 2048
TM_COMB = 128
TQ_NSA = 128
NSA_SEL_SPLIT = 4
TQ_POOL = 256
TQ_GLA = 128


def _cparams(*sem):
    return pltpu.CompilerParams(dimension_semantics=sem, vmem_limit_bytes=VMEM_LIMIT)


def _nt_dot(a, b):
    return lax.dot_general(a, b, (((1,), (1,)), ((), ())), preferred_element_type=F32)


def _dot(a, b):
    return jnp.dot(a, b, preferred_element_type=F32)


def _split3(x):
    hi = x.astype(BF16)
    r1 = x - hi.astype(F32)
    mid = r1.astype(BF16)
    lo = (r1 - mid.astype(F32)).astype(BF16)
    return hi, mid, lo


def _sigmoid(x):
    return 1.0 / (1.0 + jnp.exp(-x))


def _div_pow2(x, d):
    assert d & (d - 1) == 0
    return jnp.right_shift(x, d.bit_length() - 1)


def _mm_body(*refs, rope, tn):
    if rope:
        x_ref, w_ref, cos_ref, sin_ref, o_ref, wbf_ref = refs
    else:
        x_ref, w_ref, o_ref, wbf_ref = refs
    n = pl.program_id(0)

    @pl.when(pl.program_id(1) == 0)
    def _():
        wbf_ref[...] = w_ref[...].astype(BF16)

    acc = _dot(x_ref[...], wbf_ref[...])
    if rope:
        n_q = NSA_Q_DIM // tn
        is_q = n < n_q
        is_k = jnp.logical_and(n >= n_q, ((n - n_q) & 1) == 0)
        do = jnp.logical_or(is_q, is_k)
        c = jnp.where(do, cos_ref[...], 1.0)
        s = jnp.where(do, sin_ref[...], 0.0)
        scale = jnp.where(is_q, HEAD_DIM ** -0.5, 1.0)
        for h in range(tn // HEAD_DIM):
            xh = acc[:, h * HEAD_DIM:(h + 1) * HEAD_DIM]
            o_ref[:, h * HEAD_DIM:(h + 1) * HEAD_DIM] = (xh * c + pltpu.roll(xh, HEAD_DIM // 2, 1) * s) * scale
    else:
        o_ref[...] = acc.astype(o_ref.dtype)


def _mm(x_bf, w, *, n_out, tn, layer=None, rope_tabs=None, tm=TM_DENSE):
    mp, k = x_bf.shape
    assert mp % tm == 0 and n_out % tn == 0
    if layer is None:
        w_spec = pl.BlockSpec((k, tn), lambda n, m: (0, n))
    else:
        w_spec = pl.BlockSpec((None, k, tn), lambda n, m: (layer, 0, n))
    in_specs = [pl.BlockSpec((tm, k), lambda n, m: (m, 0)), w_spec]
    args = [x_bf, w]
    if rope_tabs is not None:
        in_specs += [pl.BlockSpec((tm, HEAD_DIM), lambda n, m: (m, 0))] * 2
        args += list(rope_tabs)
    return pl.pallas_call(
        functools.partial(_mm_body, rope=rope_tabs is not None, tn=tn),
        grid=(n_out // tn, mp // tm),
        in_specs=in_specs,
        out_specs=pl.BlockSpec((tm, tn), lambda n, m: (m, n)),
        out_shape=jax.ShapeDtypeStruct((mp, n_out), F32),
        scratch_shapes=[pltpu.VMEM((k, tn), BF16)],
        compiler_params=_cparams("arbitrary", "arbitrary"),
        name="mm_rope" if rope_tabs is not None else "mm",
    )(*args)


def _layer_norm_rows(v, g, b):
    mu = jnp.mean(v, axis=-1, keepdims=True)
    d = v - mu
    var = jnp.mean(d * d, axis=-1, keepdims=True)
    return d * lax.rsqrt(var + LN_EPS) * g + b


def _ln_body(x_ref, y_ref, g_ref, b_ref, of_ref, ob_ref):
    o = _layer_norm_rows(DN_ALPHA * x_ref[...] + y_ref[...], g_ref[...], b_ref[...])
    of_ref[...] = o
    ob_ref[...] = o.astype(BF16)


def _ln_res(x, y, g, b, *, tm=TM_ROW):
    mp, d = x.shape
    row = pl.BlockSpec((tm, d), lambda m: (m, 0))
    vec = pl.BlockSpec((1, d), lambda m: (0, 0))
    return pl.pallas_call(
        _ln_body,
        grid=(mp // tm,),
        in_specs=[row, row, vec, vec],
        out_specs=[row, row],
        out_shape=[jax.ShapeDtypeStruct((mp, d), F32), jax.ShapeDtypeStruct((mp, d), BF16)],
        compiler_params=_cparams("parallel"),
        name="ln_res",
    )(x, y, g.reshape(1, d), b.reshape(1, d))


def _comb_body(x_ref, y4_ref, gate_ref, g_ref, b_ref, of_ref, ob_ref):
    gate = gate_ref[...]
    y = gate[:, 0:1] * y4_ref[0]
    for k in range(1, TOP_K):
        y = y + gate[:, k:k + 1] * y4_ref[k]
    o = _layer_norm_rows(DN_ALPHA * x_ref[...] + y, g_ref[...], b_ref[...])
    of_ref[...] = o
    ob_ref[...] = o.astype(BF16)


def _moe_combine_ln(x, y4, gate, g, b, *, tm=TM_COMB):
    mp, d = x.shape
    row = pl.BlockSpec((tm, d), lambda m: (m, 0))
    vec = pl.BlockSpec((1, d), lambda m: (0, 0))
    return pl.pallas_call(
        _comb_body,
        grid=(mp // tm,),
        in_specs=[row, pl.BlockSpec((TOP_K, tm, d), lambda m: (0, m, 0)),
                  pl.BlockSpec((tm, TOP_K), lambda m: (m, 0)), vec, vec],
        out_specs=[row, row],
        out_shape=[jax.ShapeDtypeStruct((mp, d), F32), jax.ShapeDtypeStruct((mp, d), BF16)],
        compiler_params=_cparams("parallel"),
        name="moe_combine_ln",
    )(x, y4, gate, g.reshape(1, d), b.reshape(1, d))


def _gelu_tanh(x):
    return 0.5 * x * (1.0 + jnp.tanh(np.sqrt(2.0 / np.pi) * (x + 0.044715 * (x * x * x))))


def _cmp_body(x_ref, w1_ref, w2_ref, o_ref, *, nch, nco):
    x = x_ref[...].astype(BF16)
    first = _dot(x, w1_ref[0].astype(BF16))
    second = _dot(x, w1_ref[1].astype(BF16))
    hid = _gelu_tanh(first + pltpu.roll(second, nch - 1, 0))
    o_ref[0:nch, :] = _dot(hid.astype(BF16), w2_ref[...].astype(BF16))
    if nco > nch:
        o_ref[nch:nco, :] = jnp.zeros((nco - nch, HEAD_DIM), F32)


def _compress(rows, w1, w2, *, nco):
    b, t, _ = rows.shape
    nch = t // CMP_STRIDE
    assert nch % 8 == 0 and nco >= nch
    xt = rows.reshape(b, nch, CMP_STRIDE, 2, NSA_KV_HEADS, HEAD_DIM)
    xt = jnp.transpose(xt, (0, 3, 4, 1, 2, 5)).reshape(b, 2, NSA_KV_HEADS, nch, CMP_STRIDE * HEAD_DIM)
    w1r = w1.reshape(2, CMP_BLOCK // CMP_STRIDE, CMP_STRIDE * HEAD_DIM, HEAD_DIM)
    return pl.pallas_call(
        functools.partial(_cmp_body, nch=nch, nco=nco),
        grid=(b, 2, NSA_KV_HEADS),
        in_specs=[pl.BlockSpec((None, None, None, nch, CMP_STRIDE * HEAD_DIM), lambda bi, c, g: (bi, c, g, 0, 0)),
                  pl.BlockSpec((None, 2, CMP_STRIDE * HEAD_DIM, HEAD_DIM), lambda bi, c, g: (c, 0, 0, 0)),
                  pl.BlockSpec((None, HEAD_DIM, HEAD_DIM), lambda bi, c, g: (c, 0, 0))],
        out_specs=pl.BlockSpec((None, None, None, nco, HEAD_DIM), lambda bi, c, g: (bi, c, g, 0, 0)),
        out_shape=jax.ShapeDtypeStruct((b, 2, NSA_KV_HEADS, nco, HEAD_DIM), F32),
        compiler_params=_cparams("parallel", "parallel", "parallel"),
        name="nsa_compress",
    )(xt, w1r, w2)


def _softmax_pv(s, mask, v_bf):
    s = jnp.where(mask, s, NEG_INF)
    m = jnp.max(s, axis=1, keepdims=True)
    e = jnp.exp(s - m)
    den = jnp.sum(e, axis=1, keepdims=True)
    return _dot((e / den).astype(BF16), v_bf)


def _nsa_body(q_ref, gt_ref, ck_ref, cv_ref, ks_ref, vs_ref, kw_ref, vw_ref, o_ref, *scratch,
              tq, pos_base, n_cmp, n_sel, nwk, kw_base, win_slide, sel_split):
    g = pl.program_id(1)
    i = pl.program_id(2)
    rows = NSA_GROUP * tq
    pos0 = pos_base + i * tq
    q = jnp.concatenate([q_ref[:, j * HEAD_DIM:(j + 1) * HEAD_DIM] for j in range(NSA_GROUP)], axis=0).astype(BF16)
    pos = pos0 + (lax.broadcasted_iota(jnp.int32, (rows, 1), 0) & (tq - 1))

    ck = ck_ref[...].astype(BF16)
    ncp = ck.shape[0]
    n_i = lax.broadcasted_iota(jnp.int32, (1, ncp), 1)
    vis = jnp.logical_and(n_i * CMP_STRIDE + (CMP_BLOCK - 1) <= pos, n_i < n_cmp)
    s = jnp.where(vis, _nt_dot(q, ck), NEG_INF)
    e = jnp.where(vis, jnp.exp(s - jnp.max(s, axis=1, keepdims=True)), 0.0)
    den = jnp.sum(e, axis=1, keepdims=True)
    p = e / jnp.where(den > 0.0, den, 1.0)
    p_bf = p.astype(BF16)
    o_cmp = _dot(p_bf, cv_ref[...].astype(BF16))

    nsb = -(-n_sel // LANE) * LANE
    per = SEL_BLOCK // CMP_STRIDE
    n_c = lax.broadcasted_iota(jnp.int32, (ncp, nsb), 0)
    s_c = lax.broadcasted_iota(jnp.int32, (ncp, nsb), 1)
    share = jnp.zeros((ncp, nsb), F32)
    for r in range(CMP_BLOCK // CMP_STRIDE):
        share = share + jnp.where(_div_pow2(n_c + r, per) == s_c, 1.0, 0.0)
    imp_rows = _dot(p_bf, share.astype(BF16))
    imp = imp_rows[0:tq]
    for j in range(1, NSA_GROUP):
        imp = imp + imp_rows[j * tq:(j + 1) * tq]
    blk = lax.broadcasted_iota(jnp.int32, (1, nsb), 1)
    tpos = pos[0:tq]
    cur = _div_pow2(tpos, SEL_BLOCK)
    forced = jnp.logical_or(blk == 0, jnp.logical_or(blk == cur, blk == cur - 1))
    score = jnp.where(forced, FORCE_SCORE, jnp.where(blk * SEL_BLOCK <= tpos, imp, -1.0))
    score = jnp.where(blk < n_sel, score, -2.0)
    rank = jnp.zeros((tq, nsb), F32)
    for sp in range(n_sel):
        col = score[:, sp:sp + 1]
        ahead = jnp.logical_or(col > score, jnp.logical_and(col == score, blk > sp))
        rank = rank + jnp.where(ahead, 1.0, 0.0)
    chosen = jnp.where(jnp.logical_and(rank < min(SEL_TOP_N, n_sel), blk < n_sel), 1.0, 0.0).astype(BF16)

    def sel_branch(nk):
        sb_i = lax.broadcasted_iota(jnp.int32, (nsb, nk), 0)
        k_i = lax.broadcasted_iota(jnp.int32, (nsb, nk), 1)
        expand = jnp.where(_div_pow2(k_i, SEL_BLOCK) == sb_i, 1.0, 0.0).astype(BF16)
        in_blk = _dot(chosen, expand)
        in_blk = jnp.concatenate([in_blk] * NSA_GROUP, axis=0)
        kpos = lax.broadcasted_iota(jnp.int32, (1, nk), 1)
        ok = jnp.logical_and(in_blk > 0.5, kpos <= pos)
        return _softmax_pv(_nt_dot(q, ks_ref[0:nk, :].astype(BF16)), ok, vs_ref[0:nk, :].astype(BF16))

    nsk = ks_ref.shape[0]
    if sel_split == 1:
        o_sel = sel_branch(nsk)
    else:
        osel_ref, = scratch
        step = nsk // sel_split
        tile_end = pos0 + tq
        for c in range(sel_split):
            @pl.when(jnp.logical_and(tile_end > c * step, tile_end <= (c + 1) * step))
            def _(c=c):
                osel_ref[...] = sel_branch((c + 1) * step)
        o_sel = osel_ref[...]

    if win_slide:
        wk0 = pl.multiple_of(jnp.clip(pos0 - WINDOW, 0, kw_ref.shape[0] - nwk), LANE)
        kw = kw_ref[pl.ds(wk0, nwk), :]
        vw = vw_ref[pl.ds(wk0, nwk), :]
    else:
        wk0 = 0
        kw = kw_ref[...]
        vw = vw_ref[...]
    kposw = kw_base + wk0 + lax.broadcasted_iota(jnp.int32, (1, nwk), 1)
    okw = jnp.logical_and(kposw <= pos, kposw > pos - WINDOW)
    o_win = _softmax_pv(_nt_dot(q, kw.astype(BF16)), okw, vw.astype(BF16))

    gt = gt_ref[...]
    lane = lax.broadcasted_iota(jnp.int32, (1, LANE), 1)

    def gate(c):
        cols = []
        for j in range(NSA_GROUP):
            col = 3 * (NSA_GROUP * g + j) + c
            cols.append(_sigmoid(jnp.sum(jnp.where(lane == col, gt, 0.0), axis=1, keepdims=True)))
        return jnp.concatenate(cols, axis=0)

    o = gate(0) * o_cmp + gate(1) * o_sel + gate(2) * o_win
    for j in range(NSA_GROUP):
        o_ref[:, j * HEAD_DIM:(j + 1) * HEAD_DIM] = o[j * tq:(j + 1) * tq].astype(o_ref.dtype)


def _nsa_attend(q2d, gates2d, comp, kv_sel, kv_win, *, b, t, tq, pos_base, n_cmp, n_sel, nsk, nwt, nwk, kw_base,
                win_slide, q_rows_per_batch, sel_cols, win_cols):
    nq = t // tq
    assert q_rows_per_batch % tq == 0
    rb = q_rows_per_batch // tq
    ncp = comp.shape[3]
    sel_split = NSA_SEL_SPLIT if (pos_base == 0 and nsk % (NSA_SEL_SPLIT * LANE) == 0) else 1
    body = functools.partial(_nsa_body, tq=tq, pos_base=pos_base, n_cmp=n_cmp, n_sel=n_sel, nwk=nwk,
                             kw_base=kw_base, win_slide=win_slide, sel_split=sel_split)

    def kv_spec(arr, keys, col):
        if arr.ndim == 2:
            return pl.BlockSpec((keys, HEAD_DIM), lambda bi, g, i: (bi, col + g))
        return pl.BlockSpec((None, keys, HEAD_DIM), lambda bi, g, i: (bi, 0, col + g))

    return pl.pallas_call(
        body,
        grid=(b, NSA_KV_HEADS, nq),
        in_specs=[pl.BlockSpec((tq, NSA_GROUP * HEAD_DIM), lambda bi, g, i: (bi * rb + i, g)),
                  pl.BlockSpec((tq, LANE), lambda bi, g, i: (bi * rb + i, 0)),
                  pl.BlockSpec((None, None, None, ncp, HEAD_DIM), lambda bi, g, i: (bi, 0, g, 0, 0)),
                  pl.BlockSpec((None, None, None, ncp, HEAD_DIM), lambda bi, g, i: (bi, 1, g, 0, 0)),
                  kv_spec(kv_sel, nsk, sel_cols[0]), kv_spec(kv_sel, nsk, sel_cols[1]),
                  kv_spec(kv_win, nwt, win_cols[0]), kv_spec(kv_win, nwt, win_cols[1])],
        out_specs=pl.BlockSpec((tq, NSA_GROUP * HEAD_DIM), lambda bi, g, i: (bi * nq + i, g)),
        out_shape=jax.ShapeDtypeStruct((b * t, NSA_Q_DIM), BF16),
        scratch_shapes=[pltpu.VMEM((NSA_GROUP * tq, HEAD_DIM), F32)] if sel_split > 1 else [],
        compiler_params=_cparams("parallel", "parallel", "arbitrary"),
        name="nsa_attend",
    )(q2d, gates2d, comp, comp, kv_sel, kv_sel, kv_win, kv_win)


def _pool_body(u_ref, halo_ref, prev_ref, w_ref, sc_ref, o_ref, ext_ref, *, tq, pos_base, n_tiles):
    i = pl.program_id(1)
    ext_ref[POOL_HALO:POOL_HALO + tq, :] = u_ref[...]
    if n_tiles > 1:
        @pl.when(i == 0)
        def _():
            ext_ref[0:POOL_HALO, :] = prev_ref[...]

        @pl.when(i > 0)
        def _():
            ext_ref[0:POOL_HALO, :] = halo_ref[...]
    else:
        ext_ref[0:POOL_HALO, :] = prev_ref[...]
    n_pos = pos_base + i * tq + lax.broadcasted_iota(jnp.int32, (tq, 1), 0) + 1
    for gi, w in enumerate(POOL_WINDOWS):
        cols = slice(gi * POOL_GROUP_DIM, (gi + 1) * POOL_GROUP_DIM)
        cur = ext_ref[POOL_HALO:POOL_HALO + tq, cols]
        acc = cur
        for k in range(1, w):
            acc = acc + ext_ref[POOL_HALO - k:POOL_HALO - k + tq, cols]
        d = acc / jnp.minimum(n_pos, w).astype(F32) - cur
        o_ref[:, cols] = (_dot(d.astype(BF16), w_ref[gi]) * sc_ref[:, cols]).astype(o_ref.dtype)


def _pool_mix(u2d, prev, w_bf, scale, *, b, t, tq, pos_base, rows_per_batch):
    nt = t // tq
    rb = rows_per_batch // tq
    hb = tq // POOL_HALO if nt > 1 else 1

    def halo_map(bi, i):
        return (jnp.maximum((bi * rb + i) * hb - 1, 0), 0)

    halo_rows = POOL_HALO if nt > 1 else tq
    return pl.pallas_call(
        functools.partial(_pool_body, tq=tq, pos_base=pos_base, n_tiles=nt),
        grid=(b, nt),
        in_specs=[pl.BlockSpec((tq, POOL_DIM), lambda bi, i: (bi * rb + i, 0)),
                  pl.BlockSpec((halo_rows, POOL_DIM), halo_map),
                  pl.BlockSpec((None, POOL_HALO, POOL_DIM), lambda bi, i: (bi, 0, 0)),
                  pl.BlockSpec((len(POOL_WINDOWS), POOL_GROUP_DIM, POOL_GROUP_DIM), lambda bi, i: (0, 0, 0)),
                  pl.BlockSpec((1, POOL_DIM), lambda bi, i: (0, 0))],
        out_specs=pl.BlockSpec((tq, POOL_DIM), lambda bi, i: (bi * nt + i, 0)),
        out_shape=jax.ShapeDtypeStruct((b * t, POOL_DIM), BF16),
        scratch_shapes=[pltpu.VMEM((POOL_HALO + tq, POOL_DIM), F32)],
        compiler_params=_cparams("parallel", "arbitrary"),
        name="pool_mix",
    )(u2d, u2d, prev, w_bf, scale.reshape(1, POOL_DIM))


def _gla_body(q_ref, k_ref, v_ref, gg_ref, lr_ref, wg2_ref, bg_ref, ng_ref, s0_ref, o_ref, sout_ref, st_ref, *,
              tq, t_valid, n_tiles):
    i = pl.program_id(2)

    @pl.when(i == 0)
    def _():
        st_ref[...] = s0_ref[...].T

    z = _dot(lr_ref[...].astype(BF16), wg2_ref[...]) + bg_ref[...]
    log_a = (jnp.minimum(z, 0.0) - jnp.log1p(jnp.exp(-jnp.abs(z)))) * (1.0 / GLA_GATE_NORM)
    t_i = i * tq + lax.broadcasted_iota(jnp.int32, (tq, 1), 0)
    log_a = jnp.where(t_i < t_valid, log_a, 0.0)
    r_i = lax.broadcasted_iota(jnp.int32, (tq, tq), 0)
    c_i = lax.broadcasted_iota(jnp.int32, (tq, tq), 1)
    tri = jnp.where(c_i <= r_i, 1.0, 0.0).astype(BF16)
    cum = sum(_dot(tri, part) for part in _split3(log_a))
    cum_end = cum[tq - 1:tq, :]

    q = q_ref[...] * (GLA_DK ** -0.5)
    k = k_ref[...]
    v_bf = v_ref[...].astype(BF16)
    st = st_ref[...]
    o = _nt_dot((q * jnp.exp(cum)).astype(BF16), st.astype(BF16))
    att_rows = []
    for a in range(tq // GLA_CHUNK):
        lo, hi = a * GLA_CHUNK, (a + 1) * GLA_CHUNK
        base = cum[lo - 1:lo, :] if a > 0 else jnp.zeros((1, GLA_DK), F32)
        qa = (q[lo:hi] * jnp.exp(cum[lo:hi] - base)).astype(BF16)
        s_i = lax.broadcasted_iota(jnp.int32, (tq, 1), 0)
        ka = (k * jnp.exp(jnp.where(s_i < hi, base - cum, 0.0))).astype(BF16)
        att = _nt_dot(qa, ka)
        row = lo + lax.broadcasted_iota(jnp.int32, (GLA_CHUNK, tq), 0)
        col = lax.broadcasted_iota(jnp.int32, (GLA_CHUNK, tq), 1)
        att_rows.append(jnp.where(col <= row, att, 0.0))
    att = jnp.concatenate(att_rows, axis=0) if len(att_rows) > 1 else att_rows[0]
    o = o + _dot(att.astype(BF16), v_bf)
    kd = (k * jnp.exp(cum_end - cum)).astype(BF16)
    st_new = st * jnp.exp(cum_end) + lax.dot_general(v_bf, kd, (((0,), (0,)), ((), ())), preferred_element_type=F32)
    st_ref[...] = st_new

    @pl.when(i == n_tiles - 1)
    def _():
        sout_ref[...] = st_new.T

    o = o * lax.rsqrt(jnp.mean(o * o, axis=-1, keepdims=True) + RMS_EPS) * ng_ref[...]
    gg = gg_ref[...]
    o_ref[...] = (o * (gg * _sigmoid(gg))).astype(o_ref.dtype)


def _gla(proj2d, lr2d, wg2_bf, b_gate, norm_g, s0, *, b, t, tq, t_valid, rows_per_batch):
    nt = t // tq
    rb = rows_per_batch // tq
    h_ = GLA_HEADS
    body = functools.partial(_gla_body, tq=tq, t_valid=t_valid, n_tiles=nt)
    kq = GLA_DK
    return pl.pallas_call(
        body,
        grid=(b, h_, nt),
        in_specs=[pl.BlockSpec((tq, kq), lambda bi, h, i: (bi * rb + i, h)),
                  pl.BlockSpec((tq, kq), lambda bi, h, i: (bi * rb + i, h_ + h)),
                  pl.BlockSpec((tq, GLA_DV), lambda bi, h, i: (bi * rb + i, h_ + h)),
                  pl.BlockSpec((tq, GLA_DV), lambda bi, h, i: (bi * rb + i, 2 * h_ + h)),
                  pl.BlockSpec((tq, LANE), lambda bi, h, i: (bi * rb + i, 0)),
                  pl.BlockSpec((LANE, kq), lambda bi, h, i: (0, h)),
                  pl.BlockSpec((1, kq), lambda bi, h, i: (0, h)),
                  pl.BlockSpec((1, GLA_DV), lambda bi, h, i: (0, 0)),
                  pl.BlockSpec((None, None, kq, GLA_DV), lambda bi, h, i: (bi, h, 0, 0))],
        out_specs=[pl.BlockSpec((tq, GLA_DV), lambda bi, h, i: (bi * nt + i, h)),
                   pl.BlockSpec((None, None, kq, GLA_DV), lambda bi, h, i: (bi, h, 0, 0))],
        out_shape=[jax.ShapeDtypeStruct((b * t, h_ * GLA_DV), BF16),
                   jax.ShapeDtypeStruct((b, h_, kq, GLA_DV), F32)],
        scratch_shapes=[pltpu.VMEM((GLA_DV, kq), F32)],
        compiler_params=_cparams("parallel", "parallel", "arbitrary"),
        name="gla",
    )(proj2d, proj2d, proj2d, proj2d, lr2d, wg2_bf, b_gate.reshape(1, -1), norm_g.reshape(1, -1), s0)


def _router_body(x_ref, w_ref, b_ref, o_ref):
    o_ref[...] = _dot(x_ref[...], w_ref[...].astype(BF16)) + b_ref[...]


def _router(x_bf, w_pad, b_pad, *, tm=TM_ROW):
    mp, d = x_bf.shape
    return pl.pallas_call(
        _router_body,
        grid=(mp // tm,),
        in_specs=[pl.BlockSpec((tm, d), lambda m: (m, 0)),
                  pl.BlockSpec((d, LANE), lambda m: (0, 0)),
                  pl.BlockSpec((1, LANE), lambda m: (0, 0))],
        out_specs=pl.BlockSpec((tm, LANE), lambda m: (m, 0)),
        out_shape=jax.ShapeDtypeStruct((mp, LANE), F32),
        compiler_params=_cparams("parallel"),
        name="moe_router",
    )(x_bf, w_pad, b_pad)


def _block_ids(be_ref, nu_ref):
    r = pl.program_id(1)
    n_used = nu_ref[0]
    rc = jnp.minimum(r, n_used - 1)
    live = r < n_used
    fresh = jnp.logical_or(r == 0, jnp.logical_and(live, be_ref[rc] != be_ref[jnp.maximum(rc - 1, 0)]))
    return live, fresh


def _ffn_up_body(be_ref, nu_ref, xs_ref, wg_ref, wu_ref, bg_ref, bu_ref, h_ref, wg_bf, wu_bf):
    live, fresh = _block_ids(be_ref, nu_ref)

    @pl.when(fresh)
    def _():
        wg_bf[...] = wg_ref[...].astype(BF16)
        wu_bf[...] = wu_ref[...].astype(BF16)

    @pl.when(live)
    def _():
        x = xs_ref[...]
        hg = jnp.minimum(_dot(x, wg_bf[...]) + bg_ref[...], SWIGLU_LIMIT)
        hu = jnp.clip(_dot(x, wu_bf[...]) + bu_ref[...], -SWIGLU_LIMIT, SWIGLU_LIMIT)
        h_ref[...] = (hg * _sigmoid(SWIGLU_ALPHA * hg) * (hu + 1.0)).astype(h_ref.dtype)


def _ffn_down_body(be_ref, nu_ref, h_ref, wd_ref, bd_ref, y_ref, wd_bf):
    live, fresh = _block_ids(be_ref, nu_ref)

    @pl.when(fresh)
    def _():
        wd_bf[...] = wd_ref[...].astype(BF16)

    @pl.when(live)
    def _():
        y_ref[...] = _dot(h_ref[...], wd_bf[...]) + bd_ref[...]


def _moe_ffn(xs, blk_exp, n_used, layer, w_g, b_g, w_u, b_u, w_d, b_d, *, tm=TM_MOE, tf=TF_MOE, tn=TN_MOE):
    cap, d = xs.shape
    n_blk = cap // tm

    def blk(r, nu):
        return jnp.minimum(r, nu[0] - 1)

    h = pl.pallas_call(
        _ffn_up_body,
        grid_spec=pltpu.PrefetchScalarGridSpec(
            num_scalar_prefetch=2,
            grid=(D_FF // tf, n_blk),
            in_specs=[pl.BlockSpec((tm, d), lambda f, r, be, nu: (blk(r, nu), 0)),
                      pl.BlockSpec((None, None, d, tf), lambda f, r, be, nu: (layer, be[blk(r, nu)], 0, f)),
                      pl.BlockSpec((None, None, d, tf), lambda f, r, be, nu: (layer, be[blk(r, nu)], 0, f)),
                      pl.BlockSpec((None, None, 1, tf), lambda f, r, be, nu: (layer, be[blk(r, nu)], 0, f)),
                      pl.BlockSpec((None, None, 1, tf), lambda f, r, be, nu: (layer, be[blk(r, nu)], 0, f))],
            out_specs=pl.BlockSpec((tm, tf), lambda f, r, be, nu: (blk(r, nu), f)),
            scratch_shapes=[pltpu.VMEM((d, tf), BF16), pltpu.VMEM((d, tf), BF16)]),
        out_shape=jax.ShapeDtypeStruct((cap, D_FF), BF16),
        compiler_params=_cparams("arbitrary", "arbitrary"),
        name="moe_ffn_up",
    )(blk_exp, n_used, xs, w_g, w_u, b_g.reshape(DEPTH, N_EXPERTS, 1, D_FF), b_u.reshape(DEPTH, N_EXPERTS, 1, D_FF))
    return pl.pallas_call(
        _ffn_down_body,
        grid_spec=pltpu.PrefetchScalarGridSpec(
            num_scalar_prefetch=2,
            grid=(d // tn, n_blk),
            in_specs=[pl.BlockSpec((tm, D_FF), lambda n, r, be, nu: (blk(r, nu), 0)),
                      pl.BlockSpec((None, None, D_FF, tn), lambda n, r, be, nu: (layer, be[blk(r, nu)], 0, n)),
                      pl.BlockSpec((None, None, 1, tn), lambda n, r, be, nu: (layer, be[blk(r, nu)], 0, n))],
            out_specs=pl.BlockSpec((tm, tn), lambda n, r, be, nu: (blk(r, nu), n)),
            scratch_shapes=[pltpu.VMEM((D_FF, tn), BF16)]),
        out_shape=jax.ShapeDtypeStruct((cap, d), F32),
        compiler_params=_cparams("arbitrary", "arbitrary"),
        name="moe_ffn_down",
    )(blk_exp, n_used, h, w_d, b_d.reshape(DEPTH, N_EXPERTS, 1, d))


def _routing_tables(top_exp, n_tok, tm):
    n_slot = n_tok * TOP_K
    n_blk = -(-n_slot // tm) + N_EXPERTS
    flat_e = top_exp.reshape(-1).astype(jnp.int32)
    slot_i = jnp.arange(n_slot, dtype=jnp.int32)
    order = jnp.argsort(flat_e * n_slot + slot_i).astype(jnp.int32)
    experts = jnp.arange(N_EXPERTS, dtype=jnp.int32)
    counts = jnp.sum((flat_e[:, None] == experts[None, :]).astype(jnp.int32), axis=0)
    padded = (counts + tm - 1) // tm * tm
    start = jnp.cumsum(counts) - counts
    pend = jnp.cumsum(padded)
    pstart = pend - padded
    e_sorted = flat_e[order]
    dest_sorted = pstart[e_sorted] + slot_i - start[e_sorted]
    slot_row = dest_sorted[jnp.argsort(order)].reshape(n_tok, TOP_K)
    blk_first = jnp.arange(n_blk, dtype=jnp.int32) * tm
    blk_exp = jnp.minimum(jnp.sum((pend[None, :] <= blk_first[:, None]).astype(jnp.int32), axis=1), N_EXPERTS - 1)
    n_used = pend[-1:] // tm
    row_e = jnp.repeat(blk_exp, tm)
    off = jnp.arange(n_blk * tm, dtype=jnp.int32) - pstart[row_e]
    live = jnp.logical_and(off < counts[row_e], jnp.repeat(blk_first < pend[-1], tm))
    slot_tok = jnp.where(live, order[jnp.clip(start[row_e] + off, 0, n_slot - 1)] // TOP_K, 0)
    return slot_tok.astype(jnp.int32), blk_exp.astype(jnp.int32), n_used.astype(jnp.int32), slot_row


def _moe_layer(layer, x, x_bf, n_tok, w_r, b_r, w_g, b_g, w_u, b_u, w_d, b_d, ln_g, ln_b, *, tm=TM_MOE):
    mp, d = x.shape
    w_pad = jnp.pad(w_r[layer], ((0, 0), (0, LANE - N_EXPERTS)))
    b_pad = jnp.pad(b_r[layer], (0, LANE - N_EXPERTS)).reshape(1, LANE)
    logits = _router(x_bf, w_pad, b_pad)[:n_tok, :N_EXPERTS]
    top_val, top_exp = lax.top_k(logits, TOP_K)
    gate = jax.nn.softmax(top_val, axis=-1)
    slot_tok, blk_exp, n_used, slot_row = _routing_tables(top_exp, n_tok, tm)
    y = _moe_ffn(x_bf[slot_tok], blk_exp, n_used, layer, w_g, b_g, w_u, b_u, w_d, b_d, tm=tm)
    rows = jnp.pad(slot_row, ((0, mp - n_tok), (0, 0))).T
    gate_p = jnp.pad(gate, ((0, mp - n_tok), (0, 0)))
    return _moe_combine_ln(x, y[rows], gate_p, ln_g[layer, 1], ln_b[layer, 1])


def _rope_tables(pos):
    half = HEAD_DIM // 2
    inv = ROPE_THETA ** (-2.0 * jnp.arange(half, dtype=F32) / HEAD_DIM)
    ang = pos.astype(F32)[:, None] * inv[None, :]
    cos, sin = jnp.cos(ang), jnp.sin(ang)
    return jnp.concatenate([cos, cos], axis=1), jnp.concatenate([-sin, sin], axis=1)


def _pad_rows(a, rows):
    return jnp.pad(a, ((0, rows - a.shape[0]),) + ((0, 0),) * (a.ndim - 1))


def _even_layer(i, x_bf, dims, tabs, cache_cmp, cache_sel, win_buf, pool_buf, page_table,
                w_in, w_c1, w_c2, pool_w, pool_scale, w_out):
    b, s, db, ds, past = dims
    mp = x_bf.shape[0]
    n_p = b * s
    dsp = 8
    proj = _mm(x_bf, w_in, layer=i, n_out=EVEN_MAIN, tn=TN_DENSE, rope_tabs=tabs)
    u = _mm(x_bf, w_in[i, :, EVEN_MAIN + EVEN_GATES:], n_out=POOL_DIM, tn=TN_DENSE)
    gates = _mm(x_bf, jnp.pad(w_in[i, :, EVEN_MAIN:EVEN_MAIN + EVEN_GATES], ((0, 0), (0, LANE - EVEN_GATES))),
                n_out=LANE, tn=LANE)
    kv_p = proj[:n_p, NSA_Q_DIM:].reshape(b, s, 3, 2 * NSA_KV_DIM)
    kv_s = proj[n_p:n_p + db * ds, NSA_Q_DIM:].reshape(db, ds, 3, 2 * NSA_KV_DIM)
    u_s = u[n_p:n_p + db * ds].reshape(db, ds, POOL_DIM)
    pw_bf = pool_w[i].astype(BF16)

    def seq_pad(a):
        return jnp.pad(a.reshape(db, ds, -1), ((0, 0), (0, dsp - ds), (0, 0))).reshape(db * dsp, -1)

    comp_p = _compress(kv_p[:, :, 0], w_c1[i], w_c2[i], nco=s // CMP_STRIDE)
    kv_cols = NSA_Q_DIM // HEAD_DIM
    o_p = _nsa_attend(proj, gates, comp_p, proj, proj, b=b, t=s, tq=TQ_NSA, pos_base=0,
                      n_cmp=s // CMP_STRIDE - 1, n_sel=s // SEL_BLOCK, nsk=s, nwt=s, nwk=WINDOW + TQ_NSA, kw_base=0,
                      win_slide=True, q_rows_per_batch=s, sel_cols=(kv_cols + 8, kv_cols + 12),
                      win_cols=(kv_cols + 16, kv_cols + 20))
    y_p = _pool_mix(u, jnp.zeros((b, POOL_HALO, POOL_DIM), F32), pw_bf, pool_scale[i], b=b, t=s, tq=TQ_POOL,
                    pos_base=0, rows_per_batch=s)

    n_pages = past // PAGE_SIZE
    past_cmp = cache_cmp[i][page_table].reshape(db, past, 2 * NSA_KV_DIM)
    past_sel = cache_sel[i][page_table].reshape(db, past, 2 * NSA_KV_DIM)
    n_blk_s = (past + ds + CMP_STRIDE - 1) // CMP_STRIDE - 1
    nch_s = -(-(n_blk_s + 1) // 8) * 8
    rows_c = jnp.concatenate([past_cmp, kv_s[:, :, 0]], axis=1)
    rows_c = jnp.pad(rows_c, ((0, 0), (0, nch_s * CMP_STRIDE - past - ds), (0, 0)))
    comp_s = _compress(rows_c, w_c1[i], w_c2[i], nco=-(-nch_s // LANE) * LANE)
    n_sel_s = -(-(past + ds) // SEL_BLOCK)
    nsk_s = -(-(past + ds) // LANE) * LANE
    sel_s = jnp.pad(jnp.concatenate([past_sel, kv_s[:, :, 1]], axis=1), ((0, 0), (0, nsk_s - past - ds), (0, 0)))
    wb = win_buf.shape[2]
    nwk_s = -(-(wb + ds) // LANE) * LANE
    win_s = jnp.concatenate([win_buf[i].reshape(db, wb, 2 * NSA_KV_DIM), kv_s[:, :, 2]], axis=1)
    win_state_s = win_s[:, -wb:]
    win_s = jnp.pad(win_s, ((0, 0), (0, nwk_s - wb - ds), (0, 0)))
    o_s = _nsa_attend(seq_pad(proj[n_p:n_p + db * ds, :NSA_Q_DIM]), seq_pad(gates[n_p:n_p + db * ds]), comp_s,
                      sel_s, win_s, b=db, t=dsp, tq=dsp, pos_base=past, n_cmp=n_blk_s, n_sel=n_sel_s, nsk=nsk_s,
                      nwt=nwk_s, nwk=nwk_s, kw_base=past - wb, win_slide=False, q_rows_per_batch=dsp,
                      sel_cols=(0, 4), win_cols=(0, 4))
    prev_s = jnp.pad(pool_buf[i], ((0, 0), (POOL_HALO - POOL_BUF, 0), (0, 0)))
    y_s = _pool_mix(seq_pad(u[n_p:n_p + db * ds]), prev_s, pw_bf, pool_scale[i], b=db, t=dsp, tq=dsp,
                    pos_base=past, rows_per_batch=dsp)

    def seq_unpad(a):
        return a.reshape(db, dsp, -1)[:, :ds].reshape(db * ds, -1)

    mix = jnp.concatenate([jnp.concatenate([o_p, y_p], axis=1),
                           jnp.concatenate([seq_unpad(o_s), seq_unpad(y_s)], axis=1)], axis=0)
    y = _mm(_pad_rows(mix, mp), w_out, layer=i, n_out=D_MODEL, tn=TN_DENSE)
    kv_shape = (2, NSA_KV_HEADS, HEAD_DIM)
    wbp = min(WINDOW, past)
    win_state_p = jnp.pad(kv_p[:, :, 2], ((0, 0), (max(0, wbp - s), 0), (0, 0)))[:, -wbp:]
    pool_state_s = jnp.concatenate([pool_buf[i], u_s], axis=1)[:, -POOL_BUF:]
    states = (kv_p[:, :, 0].reshape((b, s) + kv_shape), kv_s[:, :, 0].reshape((db, ds) + kv_shape),
              kv_p[:, :, 1].reshape((b, s) + kv_shape), kv_s[:, :, 1].reshape((db, ds) + kv_shape),
              win_state_p.reshape((b, wbp) + kv_shape), win_state_s.reshape((db, wb) + kv_shape),
              u[:n_p].reshape(b, s, POOL_DIM)[:, -POOL_BUF:], pool_state_s)
    return y, states


def _odd_layer(i, x_bf, dims, state_gla, w_in, w_gate2, b_gate, norm_g, w_out):
    b, s, db, ds, _ = dims
    mp = x_bf.shape[0]
    n_p = b * s
    proj = _mm(x_bf, w_in, layer=i, n_out=ODD_MAIN, tn=TN_DENSE)
    lr = _mm(x_bf, jnp.pad(w_in[i, :, ODD_MAIN:], ((0, 0), (0, LANE - GLA_GATE_RANK))), n_out=LANE, tn=LANE)
    wg2 = jnp.pad(w_gate2[i], ((0, LANE - GLA_GATE_RANK), (0, 0))).astype(BF16)
    o_p, g_p = _gla(proj, lr, wg2, b_gate[i], norm_g[i], jnp.zeros((b, GLA_HEADS, GLA_DK, GLA_DV), F32),
                    b=b, t=s, tq=TQ_GLA, t_valid=s, rows_per_batch=s)
    dsp = GLA_CHUNK

    def seq_pad(a):
        return jnp.pad(a.reshape(db, ds, -1), ((0, 0), (0, dsp - ds), (0, 0))).reshape(db * dsp, -1)

    o_s, g_s = _gla(seq_pad(proj[n_p:n_p + db * ds]), seq_pad(lr[n_p:n_p + db * ds]), wg2, b_gate[i], norm_g[i],
                    state_gla[i], b=db, t=dsp, tq=dsp, t_valid=ds, rows_per_batch=dsp)
    o_s = o_s.reshape(db, dsp, -1)[:, :ds].reshape(db * ds, -1)
    y = _mm(_pad_rows(jnp.concatenate([o_p, o_s], axis=0), mp), w_out, layer=i, n_out=D_MODEL, tn=TN_DENSE)
    return y, (g_p, g_s)


def kernel(x_prompt, x_sample, cache_nsa_cmp, cache_nsa_sel, state_nsa_win, state_pool, state_gla, page_table,
           nsa_w_in, nsa_w_cmp1, nsa_w_cmp2, pool_w, pool_scale, even_w_out,
           gla_w_in, gla_w_gate2, gla_b_gate, gla_norm_g, odd_w_out,
           moe_w_router, moe_b_router, moe_w_gate, moe_b_gate, moe_w_up, moe_b_up, moe_w_down, moe_b_down,
           ln_g, ln_b):
    b, s, d = x_prompt.shape
    db, ds, _ = x_sample.shape
    past = page_table.shape[1] * PAGE_SIZE
    dims = (b, s, db, ds, past)
    n_p = b * s
    n_tok = n_p + db * ds
    mp_align = int(np.lcm.reduce([TM_DENSE, TM_ROW, TM_COMB]))
    mp = -(-n_tok // mp_align) * mp_align
    x = _pad_rows(jnp.concatenate([x_prompt.reshape(n_p, d), x_sample.reshape(db * ds, d)], axis=0), mp)
    x_bf = x.astype(BF16)
    pos = jnp.concatenate([jnp.tile(jnp.arange(s, dtype=jnp.int32), b),
                           jnp.tile(past + jnp.arange(ds, dtype=jnp.int32), db),
                           jnp.zeros((mp - n_tok,), jnp.int32)])
    tabs = _rope_tables(pos)
    even_states, odd_states = [], []
    for layer in range(DEPTH):
        i = layer // 2
        if layer % 2 == 0:
            y, st = _even_layer(i, x_bf, dims, tabs, cache_nsa_cmp, cache_nsa_sel, state_nsa_win, state_pool,
                                page_table, nsa_w_in, nsa_w_cmp1, nsa_w_cmp2, pool_w, pool_scale, even_w_out)
            even_states.append(st)
        else:
            y, st = _odd_layer(i, x_bf, dims, state_gla, gla_w_in, gla_w_gate2, gla_b_gate, gla_norm_g, odd_w_out)
            odd_states.append(st)
        x, x_bf = _ln_res(x, y, ln_g[layer, 0], ln_b[layer, 0])
        x, x_bf = _moe_layer(layer, x, x_bf, n_tok, moe_w_router, moe_b_router, moe_w_gate, moe_b_gate,
                             moe_w_up, moe_b_up, moe_w_down, moe_b_down, ln_g, ln_b)
    outs = [x[:n_p].reshape(b, s, d), x[n_p:n_tok].reshape(db, ds, d)]
    for k in range(8):
        outs.append(jnp.stack([st[k] for st in even_states]))
    for k in range(2):
        outs.append(jnp.stack([st[k] for st in odd_states]))
    return tuple(outs)
```

```python
import functools

import jax
import jax.numpy as jnp
import numpy as np
from jax import lax
from jax.experimental import pallas as pl
from jax.experimental.pallas import tpu as pltpu

F32 = jnp.float32
BF16 = jnp.bfloat16

D_MODEL = 4096
DEPTH = 4
PAGE_SIZE = 128
NSA_HEADS = 16
NSA_KV_HEADS = 4
NSA_GROUP = NSA_HEADS // NSA_KV_HEADS
HEAD_DIM = 128
CMP_BLOCK = 32
CMP_STRIDE = 16
SEL_BLOCK = 64
SEL_TOP_N = 16
WINDOW = 512
ROPE_THETA = 10000.0
FORCE_SCORE = 1e4
NEG_INF = -1e30
POOL_DIM = D_MODEL // 2
POOL_WINDOWS = (2, 4, 8, 16)
POOL_GROUP_DIM = POOL_DIM // len(POOL_WINDOWS)
POOL_BUF = max(POOL_WINDOWS) - 1
POOL_HALO = 16
GLA_HEADS = 4
GLA_DK = D_MODEL // (2 * GLA_HEADS)
GLA_DV = D_MODEL // GLA_HEADS
GLA_GATE_RANK = 16
GLA_GATE_NORM = 16.0
GLA_CHUNK = 32
N_EXPERTS = 32
TOP_K = 4
D_FF = D_MODEL // 4
SWIGLU_LIMIT = 7.0
SWIGLU_ALPHA = 1.702
LN_EPS = 1e-5
RMS_EPS = 1e-6
DN_ALPHA = (2 * DEPTH) ** 0.25
NSA_Q_DIM = NSA_HEADS * HEAD_DIM
NSA_KV_DIM = NSA_KV_HEADS * HEAD_DIM
EVEN_MAIN = NSA_Q_DIM + 6 * NSA_KV_DIM
EVEN_GATES = 3 * NSA_HEADS
ODD_MAIN = 2 * GLA_HEADS * GLA_DK + 2 * GLA_HEADS * GLA_DV

LANE = 128
VMEM_LIMIT = 56 * 1024 * 1024
TM_DENSE = 1408
TN_DENSE = 512
TM_ROW = 256
TM_MOE = 384
TF_MOE = 512
TN_MOE = 2048
TM_COMB = 128
TQ_NSA = 128
NSA_SEL_SPLIT = 4
TQ_POOL = 256
TQ_GLA = 128


def _cparams(*sem):
    return pltpu.CompilerParams(dimension_semantics=sem, vmem_limit_bytes=VMEM_LIMIT)


def _nt_dot(a, b):
    return lax.dot_general(a, b, (((1,), (1,)), ((), ())), preferred_element_type=F32)


def _dot(a, b):
    return jnp.dot(a, b, preferred_element_type=F32)


def _split3(x):
    hi = x.astype(BF16)
    r1 = x - hi.astype(F32)
    mid = r1.astype(BF16)
    lo = (r1 - mid.astype(F32)).astype(BF16)
    return hi, mid, lo


def _sigmoid(x):
    return 1.0 / (1.0 + jnp.exp(-x))


def _div_pow2(x, d):
    assert d & (d - 1) == 0
    return jnp.right_shift(x, d.bit_length() - 1)


def _mm_body(*refs, rope, tn):
    if rope:
        x_ref, w_ref, cos_ref, sin_ref, o_ref, wbf_ref = refs
    else:
        x_ref, w_ref, o_ref, wbf_ref = refs
    n = pl.program_id(0)

    @pl.when(pl.program_id(1) == 0)
    def _():
        wbf_ref[...] = w_ref[...].astype(BF16)

    acc = _dot(x_ref[...], wbf_ref[...])
    if rope:
        n_q = NSA_Q_DIM // tn
        is_q = n < n_q
        is_k = jnp.logical_and(n >= n_q, ((n - n_q) & 1) == 0)
        do = jnp.logical_or(is_q, is_k)
        c = jnp.where(do, cos_ref[...], 1.0)
        s = jnp.where(do, sin_ref[...], 0.0)
        scale = jnp.where(is_q, HEAD_DIM ** -0.5, 1.0)
        for h in range(tn // HEAD_DIM):
            xh = acc[:, h * HEAD_DIM:(h + 1) * HEAD_DIM]
            o_ref[:, h * HEAD_DIM:(h + 1) * HEAD_DIM] = (xh * c + pltpu.roll(xh, HEAD_DIM // 2, 1) * s) * scale
    else:
        o_ref[...] = acc.astype(o_ref.dtype)


def _mm(x_bf, w, *, n_out, tn, layer=None, rope_tabs=None, tm=TM_DENSE):
    mp, k = x_bf.shape
    assert mp % tm == 0 and n_out % tn == 0
    if layer is None:
        w_spec = pl.BlockSpec((k, tn), lambda n, m: (0, n))
    else:
        w_spec = pl.BlockSpec((None, k, tn), lambda n, m: (layer, 0, n))
    in_specs = [pl.BlockSpec((tm, k), lambda n, m: (m, 0)), w_spec]
    args = [x_bf, w]
    if rope_tabs is not None:
        in_specs += [pl.BlockSpec((tm, HEAD_DIM), lambda n, m: (m, 0))] * 2
        args += list(rope_tabs)
    return pl.pallas_call(
        functools.partial(_mm_body, rope=rope_tabs is not None, tn=tn),
        grid=(n_out // tn, mp // tm),
        in_specs=in_specs,
        out_specs=pl.BlockSpec((tm, tn), lambda n, m: (m, n)),
        out_shape=jax.ShapeDtypeStruct((mp, n_out), F32),
        scratch_shapes=[pltpu.VMEM((k, tn), BF16)],
        compiler_params=_cparams("arbitrary", "arbitrary"),
        name="mm_rope" if rope_tabs is not None else "mm",
    )(*args)


def _layer_norm_rows(v, g, b):
    mu = jnp.mean(v, axis=-1, keepdims=True)
    d = v - mu
    var = jnp.mean(d * d, axis=-1, keepdims=True)
    return d * lax.rsqrt(var + LN_EPS) * g + b


def _ln_body(x_ref, y_ref, g_ref, b_ref, of_ref, ob_ref):
    o = _layer_norm_rows(DN_ALPHA * x_ref[...] + y_ref[...], g_ref[...], b_ref[...])
    of_ref[...] = o
    ob_ref[...] = o.astype(BF16)


def _ln_res(x, y, g, b, *, tm=TM_ROW):
    mp, d = x.shape
    row = pl.BlockSpec((tm, d), lambda m: (m, 0))
    vec = pl.BlockSpec((1, d), lambda m: (0, 0))
    return pl.pallas_call(
        _ln_body,
        grid=(mp // tm,),
        in_specs=[row, row, vec, vec],
        out_specs=[row, row],
        out_shape=[jax.ShapeDtypeStruct((mp, d), F32), jax.ShapeDtypeStruct((mp, d), BF16)],
        compiler_params=_cparams("parallel"),
        name="ln_res",
    )(x, y, g.reshape(1, d), b.reshape(1, d))


def _comb_body(rows_ref, x_ref, gate_ref, g_ref, b_ref, y_hbm, of_ref, ob_ref, buf, sem):
    tm = x_ref.shape[0]

    def row_copy(k, t):
        return pltpu.make_async_copy(y_hbm.at[pl.ds(rows_ref[k, t], 1), :], buf.at[k, pl.ds(t, 1), :], sem)

    def issue(t, carry):
        for k in range(TOP_K):
            row_copy(k, t).start()
        return carry

    lax.fori_loop(0, tm, issue, 0, unroll=8)
    for k in range(TOP_K):
        pltpu.make_async_copy(y_hbm.at[pl.ds(0, tm), :], buf.at[k], sem).wait()
    gate = gate_ref[...]
    y = gate[:, 0:1] * buf[0]
    for k in range(1, TOP_K):
        y = y + gate[:, k:k + 1] * buf[k]
    o = _layer_norm_rows(DN_ALPHA * x_ref[...] + y, g_ref[...], b_ref[...])
    of_ref[...] = o
    ob_ref[...] = o.astype(BF16)


def _moe_combine_ln(x, y, rows, gate, g, b, *, tm=TM_COMB):
    mp, d = x.shape
    assert y.shape[0] >= tm
    row = pl.BlockSpec((tm, d), lambda m: (m, 0))
    vec = pl.BlockSpec((1, d), lambda m: (0, 0))
    return pl.pallas_call(
        _comb_body,
        grid=(mp // tm,),
        in_specs=[pl.BlockSpec((TOP_K, tm), lambda m: (0, m), memory_space=pltpu.SMEM), row,
                  pl.BlockSpec((tm, TOP_K), lambda m: (m, 0)), vec, vec,
                  pl.BlockSpec(memory_space=pl.ANY)],
        out_specs=[row, row],
        out_shape=[jax.ShapeDtypeStruct((mp, d), F32), jax.ShapeDtypeStruct((mp, d), BF16)],
        scratch_shapes=[pltpu.VMEM((TOP_K, tm, d), F32), pltpu.SemaphoreType.DMA(())],
        compiler_params=_cparams("arbitrary"),
        name="moe_combine_ln",
    )(rows, x, gate, g.reshape(1, d), b.reshape(1, d), y)


def _gelu_tanh(x):
    return 0.5 * x * (1.0 + jnp.tanh(np.sqrt(2.0 / np.pi) * (x + 0.044715 * (x * x * x))))


def _cmp_body(x_ref, w1_ref, w2_ref, o_ref, *, nch, nco):
    x = x_ref[...].astype(BF16)
    first = _dot(x, w1_ref[0].astype(BF16))
    second = _dot(x, w1_ref[1].astype(BF16))
    hid = _gelu_tanh(first + pltpu.roll(second, nch - 1, 0))
    o_ref[0:nch, :] = _dot(hid.astype(BF16), w2_ref[...].astype(BF16))
    if nco > nch:
        o_ref[nch:nco, :] = jnp.zeros((nco - nch, HEAD_DIM), F32)


def _compress(rows, w1, w2, *, nco):
    b, t, _ = rows.shape
    nch = t // CMP_STRIDE
    assert nch % 8 == 0 and nco >= nch
    xt = rows.reshape(b, nch, CMP_STRIDE, 2, NSA_KV_HEADS, HEAD_DIM)
    xt = jnp.transpose(xt, (0, 3, 4, 1, 2, 5)).reshape(b, 2, NSA_KV_HEADS, nch, CMP_STRIDE * HEAD_DIM)
    w1r = w1.reshape(2, CMP_BLOCK // CMP_STRIDE, CMP_STRIDE * HEAD_DIM, HEAD_DIM)
    return pl.pallas_call(
        functools.partial(_cmp_body, nch=nch, nco=nco),
        grid=(b, 2, NSA_KV_HEADS),
        in_specs=[pl.BlockSpec((None, None, None, nch, CMP_STRIDE * HEAD_DIM), lambda bi, c, g: (bi, c, g, 0, 0)),
                  pl.BlockSpec((None, 2, CMP_STRIDE * HEAD_DIM, HEAD_DIM), lambda bi, c, g: (c, 0, 0, 0)),
                  pl.BlockSpec((None, HEAD_DIM, HEAD_DIM), lambda bi, c, g: (c, 0, 0))],
        out_specs=pl.BlockSpec((None, None, None, nco, HEAD_DIM), lambda bi, c, g: (bi, c, g, 0, 0)),
        out_shape=jax.ShapeDtypeStruct((b, 2, NSA_KV_HEADS, nco, HEAD_DIM), F32),
        compiler_params=_cparams("parallel", "parallel", "parallel"),
        name="nsa_compress",
    )(xt, w1r, w2)


def _softmax_pv(s, mask, v_bf):
    s = jnp.where(mask, s, NEG_INF)
    m = jnp.max(s, axis=1, keepdims=True)
    e = jnp.exp(s - m)
    den = jnp.sum(e, axis=1, keepdims=True)
    return _dot((e / den).astype(BF16), v_bf)


def _nsa_body(q_ref, gt_ref, ck_ref, cv_ref, ks_ref, vs_ref, kw_ref, vw_ref, o_ref, *scratch,
              tq, pos_base, n_cmp, n_sel, nwk, kw_base, win_slide, sel_split):
    g = pl.program_id(1)
    i = pl.program_id(2)
    rows = NSA_GROUP * tq
    pos0 = pos_base + i * tq
    q = jnp.concatenate([q_ref[:, j * HEAD_DIM:(j + 1) * HEAD_DIM] for j in range(NSA_GROUP)], axis=0).astype(BF16)
    pos = pos0 + (lax.broadcasted_iota(jnp.int32, (rows, 1), 0) & (tq - 1))

    ck = ck_ref[...].astype(BF16)
    ncp = ck.shape[0]
    n_i = lax.broadcasted_iota(jnp.int32, (1, ncp), 1)
    vis = jnp.logical_and(n_i * CMP_STRIDE + (CMP_BLOCK - 1) <= pos, n_i < n_cmp)
    s = jnp.where(vis, _nt_dot(q, ck), NEG_INF)
    e = jnp.where(vis, jnp.exp(s - jnp.max(s, axis=1, keepdims=True)), 0.0)
    den = jnp.sum(e, axis=1, keepdims=True)
    p = e / jnp.where(den > 0.0, den, 1.0)
    p_bf = p.astype(BF16)
    o_cmp = _dot(p_bf, cv_ref[...].astype(BF16))

    nsb = -(-n_sel // LANE) * LANE
    per = SEL_BLOCK // CMP_STRIDE
    n_c = lax.broadcasted_iota(jnp.int32, (ncp, nsb), 0)
    s_c = lax.broadcasted_iota(jnp.int32, (ncp, nsb), 1)
    share = jnp.zeros((ncp, nsb), F32)
    for r in range(CMP_BLOCK // CMP_STRIDE):
        share = share + jnp.where(_div_pow2(n_c + r, per) == s_c, 1.0, 0.0)
    imp_rows = _dot(p_bf, share.astype(BF16))
    imp = imp_rows[0:tq]
    for j in range(1, NSA_GROUP):
        imp = imp + imp_rows[j * tq:(j + 1) * tq]
    blk = lax.broadcasted_iota(jnp.int32, (1, nsb), 1)
    tpos = pos[0:tq]
    cur = _div_pow2(tpos, SEL_BLOCK)
    forced = jnp.logical_or(blk == 0, jnp.logical_or(blk == cur, blk == cur - 1))
    score = jnp.where(forced, FORCE_SCORE, jnp.where(blk * SEL_BLOCK <= tpos, imp, -1.0))
    score = jnp.where(blk < n_sel, score, -2.0)
    rank = jnp.zeros((tq, nsb), F32)
    for sp in range(n_sel):
        col = score[:, sp:sp + 1]
        ahead = jnp.logical_or(col > score, jnp.logical_and(col == score, blk > sp))
        rank = rank + jnp.where(ahead, 1.0, 0.0)
    chosen = jnp.where(jnp.logical_and(rank < min(SEL_TOP_N, n_sel), blk < n_sel), 1.0, 0.0).astype(BF16)

    def sel_branch(nk):
        sb_i = lax.broadcasted_iota(jnp.int32, (nsb, nk), 0)
        k_i = lax.broadcasted_iota(jnp.int32, (nsb, nk), 1)
        expand = jnp.where(_div_pow2(k_i, SEL_BLOCK) == sb_i, 1.0, 0.0).astype(BF16)
        in_blk = _dot(chosen, expand)
        in_blk = jnp.concatenate([in_blk] * NSA_GROUP, axis=0)
        kpos = lax.broadcasted_iota(jnp.int32, (1, nk), 1)
        ok = jnp.logical_and(in_blk > 0.5, kpos <= pos)
        return _softmax_pv(_nt_dot(q, ks_ref[0:nk, :].astype(BF16)), ok, vs_ref[0:nk, :].astype(BF16))

    nsk = ks_ref.shape[0]
    if sel_split == 1:
        o_sel = sel_branch(nsk)
    else:
        osel_ref, = scratch
        step = nsk // sel_split
        tile_end = pos0 + tq
        for c in range(sel_split):
            @pl.when(jnp.logical_and(tile_end > c * step, tile_end <= (c + 1) * step))
            def _(c=c):
                osel_ref[...] = sel_branch((c + 1) * step)
        o_sel = osel_ref[...]

    if win_slide:
        wk0 = pl.multiple_of(jnp.clip(pos0 - WINDOW, 0, kw_ref.shape[0] - nwk), LANE)
        kw = kw_ref[pl.ds(wk0, nwk), :]
        vw = vw_ref[pl.ds(wk0, nwk), :]
    else:
        wk0 = 0
        kw = kw_ref[...]
        vw = vw_ref[...]
    kposw = kw_base + wk0 + lax.broadcasted_iota(jnp.int32, (1, nwk), 1)
    okw = jnp.logical_and(kposw <= pos, kposw > pos - WINDOW)
    o_win = _softmax_pv(_nt_dot(q, kw.astype(BF16)), okw, vw.astype(BF16))

    gt = gt_ref[...]
    lane = lax.broadcasted_iota(jnp.int32, (1, LANE), 1)

    def gate(c):
        cols = []
        for j in range(NSA_GROUP):
            col = 3 * (NSA_GROUP * g + j) + c
            cols.append(_sigmoid(jnp.sum(jnp.where(lane == col, gt, 0.0), axis=1, keepdims=True)))
        return jnp.concatenate(cols, axis=0)

    o = gate(0) * o_cmp + gate(1) * o_sel + gate(2) * o_win
    for j in range(NSA_GROUP):
        o_ref[:, j * HEAD_DIM:(j + 1) * HEAD_DIM] = o[j * tq:(j + 1) * tq].astype(o_ref.dtype)


def _nsa_attend(q2d, gates2d, comp, kv_sel, kv_win, *, b, t, tq, pos_base, n_cmp, n_sel, nsk, nwt, nwk, kw_base,
                win_slide, q_rows_per_batch, sel_cols, win_cols):
    nq = t // tq
    assert q_rows_per_batch % tq == 0
    rb = q_rows_per_batch // tq
    ncp = comp.shape[3]
    sel_split = NSA_SEL_SPLIT if (pos_base == 0 and nsk % (NSA_SEL_SPLIT * LANE) == 0) else 1
    body = functools.partial(_nsa_body, tq=tq, pos_base=pos_base, n_cmp=n_cmp, n_sel=n_sel, nwk=nwk,
                             kw_base=kw_base, win_slide=win_slide, sel_split=sel_split)

    def kv_spec(arr, keys, col):
        if arr.ndim == 2:
            return pl.BlockSpec((keys, HEAD_DIM), lambda bi, g, i: (bi, col + g))
        return pl.BlockSpec((None, keys, HEAD_DIM), lambda bi, g, i: (bi, 0, col + g))

    return pl.pallas_call(
        body,
        grid=(b, NSA_KV_HEADS, nq),
        in_specs=[pl.BlockSpec((tq, NSA_GROUP * HEAD_DIM), lambda bi, g, i: (bi * rb + i, g)),
                  pl.BlockSpec((tq, LANE), lambda bi, g, i: (bi * rb + i, 0)),
                  pl.BlockSpec((None, None, None, ncp, HEAD_DIM), lambda bi, g, i: (bi, 0, g, 0, 0)),
                  pl.BlockSpec((None, None, None, ncp, HEAD_DIM), lambda bi, g, i: (bi, 1, g, 0, 0)),
                  kv_spec(kv_sel, nsk, sel_cols[0]), kv_spec(kv_sel, nsk, sel_cols[1]),
                  kv_spec(kv_win, nwt, win_cols[0]), kv_spec(kv_win, nwt, win_cols[1])],
        out_specs=pl.BlockSpec((tq, NSA_GROUP * HEAD_DIM), lambda bi, g, i: (bi * nq + i, g)),
        out_shape=jax.ShapeDtypeStruct((b * t, NSA_Q_DIM), BF16),
        scratch_shapes=[pltpu.VMEM((NSA_GROUP * tq, HEAD_DIM), F32)] if sel_split > 1 else [],
        compiler_params=_cparams("parallel", "parallel", "arbitrary"),
        name="nsa_attend",
    )(q2d, gates2d, comp, comp, kv_sel, kv_sel, kv_win, kv_win)


def _pool_body(u_ref, halo_ref, prev_ref, w_ref, sc_ref, o_ref, ext_ref, *, tq, pos_base, n_tiles):
    i = pl.program_id(1)
    ext_ref[POOL_HALO:POOL_HALO + tq, :] = u_ref[...]
    if n_tiles > 1:
        @pl.when(i == 0)
        def _():
            ext_ref[0:POOL_HALO, :] = prev_ref[...]

        @pl.when(i > 0)
        def _():
            ext_ref[0:POOL_HALO, :] = halo_ref[...]
    else:
        ext_ref[0:POOL_HALO, :] = prev_ref[...]
    n_pos = pos_base + i * tq + lax.broadcasted_iota(jnp.int32, (tq, 1), 0) + 1
    for gi, w in enumerate(POOL_WINDOWS):
        cols = slice(gi * POOL_GROUP_DIM, (gi + 1) * POOL_GROUP_DIM)
        cur = ext_ref[POOL_HALO:POOL_HALO + tq, cols]
        acc = cur
        for k in range(1, w):
            acc = acc + ext_ref[POOL_HALO - k:POOL_HALO - k + tq, cols]
        d = acc / jnp.minimum(n_pos, w).astype(F32) - cur
        o_ref[:, cols] = (_dot(d.astype(BF16), w_ref[gi]) * sc_ref[:, cols]).astype(o_ref.dtype)


def _pool_mix(u2d, prev, w_bf, scale, *, b, t, tq, pos_base, rows_per_batch):
    nt = t // tq
    rb = rows_per_batch // tq
    hb = tq // POOL_HALO if nt > 1 else 1

    def halo_map(bi, i):
        return (jnp.maximum((bi * rb + i) * hb - 1, 0), 0)

    halo_rows = POOL_HALO if nt > 1 else tq
    return pl.pallas_call(
        functools.partial(_pool_body, tq=tq, pos_base=pos_base, n_tiles=nt),
        grid=(b, nt),
        in_specs=[pl.BlockSpec((tq, POOL_DIM), lambda bi, i: (bi * rb + i, 0)),
                  pl.BlockSpec((halo_rows, POOL_DIM), halo_map),
                  pl.BlockSpec((None, POOL_HALO, POOL_DIM), lambda bi, i: (bi, 0, 0)),
                  pl.BlockSpec((len(POOL_WINDOWS), POOL_GROUP_DIM, POOL_GROUP_DIM), lambda bi, i: (0, 0, 0)),
                  pl.BlockSpec((1, POOL_DIM), lambda bi, i: (0, 0))],
        out_specs=pl.BlockSpec((tq, POOL_DIM), lambda bi, i: (bi * nt + i, 0)),
        out_shape=jax.ShapeDtypeStruct((b * t, POOL_DIM), BF16),
        scratch_shapes=[pltpu.VMEM((POOL_HALO + tq, POOL_DIM), F32)],
        compiler_params=_cparams("parallel", "arbitrary"),
        name="pool_mix",
    )(u2d, u2d, prev, w_bf, scale.reshape(1, POOL_DIM))


def _gla_body(q_ref, k_ref, v_ref, gg_ref, lr_ref, wg2_ref, bg_ref, ng_ref, s0_ref, o_ref, sout_ref, st_ref, *,
              tq, t_valid, n_tiles):
    i = pl.program_id(2)

    @pl.when(i == 0)
    def _():
        st_ref[...] = s0_ref[...].T

    z = _dot(lr_ref[...].astype(BF16), wg2_ref[...]) + bg_ref[...]
    log_a = (jnp.minimum(z, 0.0) - jnp.log1p(jnp.exp(-jnp.abs(z)))) * (1.0 / GLA_GATE_NORM)
    t_i = i * tq + lax.broadcasted_iota(jnp.int32, (tq, 1), 0)
    log_a = jnp.where(t_i < t_valid, log_a, 0.0)
    r_i = lax.broadcasted_iota(jnp.int32, (tq, tq), 0)
    c_i = lax.broadcasted_iota(jnp.int32, (tq, tq), 1)
    tri = jnp.where(c_i <= r_i, 1.0, 0.0).astype(BF16)
    cum = sum(_dot(tri, part) for part in _split3(log_a))
    cum_end = cum[tq - 1:tq, :]

    q = q_ref[...] * (GLA_DK ** -0.5)
    k = k_ref[...]
    v_bf = v_ref[...].astype(BF16)
    st = st_ref[...]
    o = _nt_dot((q * jnp.exp(cum)).astype(BF16), st.astype(BF16))
    att_rows = []
    for a in range(tq // GLA_CHUNK):
        lo, hi = a * GLA_CHUNK, (a + 1) * GLA_CHUNK
        base = cum[lo - 1:lo, :] if a > 0 else jnp.zeros((1, GLA_DK), F32)
        qa = (q[lo:hi] * jnp.exp(cum[lo:hi] - base)).astype(BF16)
        s_i = lax.broadcasted_iota(jnp.int32, (tq, 1), 0)
        ka = (k * jnp.exp(jnp.where(s_i < hi, base - cum, 0.0))).astype(BF16)
        att = _nt_dot(qa, ka)
        row = lo + lax.broadcasted_iota(jnp.int32, (GLA_CHUNK, tq), 0)
        col = lax.broadcasted_iota(jnp.int32, (GLA_CHUNK, tq), 1)
        att_rows.append(jnp.where(col <= row, att, 0.0))
    att = jnp.concatenate(att_rows, axis=0) if len(att_rows) > 1 else att_rows[0]
    o = o + _dot(att.astype(BF16), v_bf)
    kd = (k * jnp.exp(cum_end - cum)).astype(BF16)
    st_new = st * jnp.exp(cum_end) + lax.dot_general(v_bf, kd, (((0,), (0,)), ((), ())), preferred_element_type=F32)
    st_ref[...] = st_new

    @pl.when(i == n_tiles - 1)
    def _():
        sout_ref[...] = st_new.T

    o = o * lax.rsqrt(jnp.mean(o * o, axis=-1, keepdims=True) + RMS_EPS) * ng_ref[...]
    gg = gg_ref[...]
    o_ref[...] = (o * (gg * _sigmoid(gg))).astype(o_ref.dtype)


def _gla(proj2d, lr2d, wg2_bf, b_gate, norm_g, s0, *, b, t, tq, t_valid, rows_per_batch):
    nt = t // tq
    rb = rows_per_batch // tq
    h_ = GLA_HEADS
    body = functools.partial(_gla_body, tq=tq, t_valid=t_valid, n_tiles=nt)
    kq = GLA_DK
    return pl.pallas_call(
        body,
        grid=(b, h_, nt),
        in_specs=[pl.BlockSpec((tq, kq), lambda bi, h, i: (bi * rb + i, h)),
                  pl.BlockSpec((tq, kq), lambda bi, h, i: (bi * rb + i, h_ + h)),
                  pl.BlockSpec((tq, GLA_DV), lambda bi, h, i: (bi * rb + i, h_ + h)),
                  pl.BlockSpec((tq, GLA_DV), lambda bi, h, i: (bi * rb + i, 2 * h_ + h)),
                  pl.BlockSpec((tq, LANE), lambda bi, h, i: (bi * rb + i, 0)),
                  pl.BlockSpec((LANE, kq), lambda bi, h, i: (0, h)),
                  pl.BlockSpec((1, kq), lambda bi, h, i: (0, h)),
                  pl.BlockSpec((1, GLA_DV), lambda bi, h, i: (0, 0)),
                  pl.BlockSpec((None, None, kq, GLA_DV), lambda bi, h, i: (bi, h, 0, 0))],
        out_specs=[pl.BlockSpec((tq, GLA_DV), lambda bi, h, i: (bi * nt + i, h)),
                   pl.BlockSpec((None, None, kq, GLA_DV), lambda bi, h, i: (bi, h, 0, 0))],
        out_shape=[jax.ShapeDtypeStruct((b * t, h_ * GLA_DV), BF16),
                   jax.ShapeDtypeStruct((b, h_, kq, GLA_DV), F32)],
        scratch_shapes=[pltpu.VMEM((GLA_DV, kq), F32)],
        compiler_params=_cparams("parallel", "parallel", "arbitrary"),
        name="gla",
    )(proj2d, proj2d, proj2d, proj2d, lr2d, wg2_bf, b_gate.reshape(1, -1), norm_g.reshape(1, -1), s0)


def _router_body(x_ref, w_ref, b_ref, o_ref):
    o_ref[...] = _dot(x_ref[...], w_ref[...].astype(BF16)) + b_ref[...]


def _router(x_bf, w_pad, b_pad, *, tm=TM_ROW):
    mp, d = x_bf.shape
    return pl.pallas_call(
        _router_body,
        grid=(mp // tm,),
        in_specs=[pl.BlockSpec((tm, d), lambda m: (m, 0)),
                  pl.BlockSpec((d, LANE), lambda m: (0, 0)),
                  pl.BlockSpec((1, LANE), lambda m: (0, 0))],
        out_specs=pl.BlockSpec((tm, LANE), lambda m: (m, 0)),
        out_shape=jax.ShapeDtypeStruct((mp, LANE), F32),
        compiler_params=_cparams("parallel"),
        name="moe_router",
    )(x_bf, w_pad, b_pad)


def _block_ids(be_ref, nu_ref):
    r = pl.program_id(1)
    n_used = nu_ref[0]
    rc = jnp.minimum(r, n_used - 1)
    live = r < n_used
    fresh = jnp.logical_or(r == 0, jnp.logical_and(live, be_ref[rc] != be_ref[jnp.maximum(rc - 1, 0)]))
    return live, fresh


def _ffn_up_body(be_ref, nu_ref, xs_ref, wg_ref, wu_ref, bg_ref, bu_ref, h_ref, wg_bf, wu_bf):
    live, fresh = _block_ids(be_ref, nu_ref)

    @pl.when(fresh)
    def _():
        wg_bf[...] = wg_ref[...].astype(BF16)
        wu_bf[...] = wu_ref[...].astype(BF16)

    @pl.when(live)
    def _():
        x = xs_ref[...]
        hg = jnp.minimum(_dot(x, wg_bf[...]) + bg_ref[...], SWIGLU_LIMIT)
        hu = jnp.clip(_dot(x, wu_bf[...]) + bu_ref[...], -SWIGLU_LIMIT, SWIGLU_LIMIT)
        h_ref[...] = (hg * _sigmoid(SWIGLU_ALPHA * hg) * (hu + 1.0)).astype(h_ref.dtype)

    @pl.when(jnp.logical_not(live))
    def _():
        h_ref[...] = jnp.zeros(h_ref.shape, h_ref.dtype)


def _ffn_down_body(be_ref, nu_ref, h_ref, wd_ref, bd_ref, y_ref, wd_bf):
    live, fresh = _block_ids(be_ref, nu_ref)

    @pl.when(fresh)
    def _():
        wd_bf[...] = wd_ref[...].astype(BF16)

    @pl.when(live)
    def _():
        y_ref[...] = _dot(h_ref[...], wd_bf[...]) + bd_ref[...]

    @pl.when(jnp.logical_not(live))
    def _():
        y_ref[...] = jnp.zeros(y_ref.shape, y_ref.dtype)


def _moe_ffn(xs, blk_exp, n_used, layer, w_g, b_g, w_u, b_u, w_d, b_d, *, tm=TM_MOE, tf=TF_MOE, tn=TN_MOE):
    cap, d = xs.shape
    n_blk = cap // tm

    def blk(r, nu):
        return jnp.minimum(r, nu[0] - 1)

    h = pl.pallas_call(
        _ffn_up_body,
        grid_spec=pltpu.PrefetchScalarGridSpec(
            num_scalar_prefetch=2,
            grid=(D_FF // tf, n_blk),
            in_specs=[pl.BlockSpec((tm, d), lambda f, r, be, nu: (blk(r, nu), 0)),
                      pl.BlockSpec((None, None, d, tf), lambda f, r, be, nu: (layer, be[blk(r, nu)], 0, f)),
                      pl.BlockSpec((None, None, d, tf), lambda f, r, be, nu: (layer, be[blk(r, nu)], 0, f)),
                      pl.BlockSpec((None, None, 1, tf), lambda f, r, be, nu: (layer, be[blk(r, nu)], 0, f)),
                      pl.BlockSpec((None, None, 1, tf), lambda f, r, be, nu: (layer, be[blk(r, nu)], 0, f))],
            out_specs=pl.BlockSpec((tm, tf), lambda f, r, be, nu: (r, f)),
            scratch_shapes=[pltpu.VMEM((d, tf), BF16), pltpu.VMEM((d, tf), BF16)]),
        out_shape=jax.ShapeDtypeStruct((cap, D_FF), BF16),
        compiler_params=_cparams("arbitrary", "arbitrary"),
        name="moe_ffn_up",
    )(blk_exp, n_used, xs, w_g, w_u, b_g.reshape(DEPTH, N_EXPERTS, 1, D_FF), b_u.reshape(DEPTH, N_EXPERTS, 1, D_FF))
    return pl.pallas_call(
        _ffn_down_body,
        grid_spec=pltpu.PrefetchScalarGridSpec(
            num_scalar_prefetch=2,
            grid=(d // tn, n_blk),
            in_specs=[pl.BlockSpec((tm, D_FF), lambda n, r, be, nu: (blk(r, nu), 0)),
                      pl.BlockSpec((None, None, D_FF, tn), lambda n, r, be, nu: (layer, be[blk(r, nu)], 0, n)),
                      pl.BlockSpec((None, None, 1, tn), lambda n, r, be, nu: (layer, be[blk(r, nu)], 0, n))],
            out_specs=pl.BlockSpec((tm, tn), lambda n, r, be, nu: (r, n)),
            scratch_shapes=[pltpu.VMEM((D_FF, tn), BF16)]),
        out_shape=jax.ShapeDtypeStruct((cap, d), F32),
        compiler_params=_cparams("arbitrary", "arbitrary"),
        name="moe_ffn_down",
    )(blk_exp, n_used, h, w_d, b_d.reshape(DEPTH, N_EXPERTS, 1, d))


def _routing_tables(top_exp, n_tok, tm):
    n_slot = n_tok * TOP_K
    n_blk = -(-n_slot // tm) + N_EXPERTS
    flat_e = top_exp.reshape(-1).astype(jnp.int32)
    slot_i = jnp.arange(n_slot, dtype=jnp.int32)
    order = jnp.argsort(flat_e * n_slot + slot_i).astype(jnp.int32)
    experts = jnp.arange(N_EXPERTS, dtype=jnp.int32)
    counts = jnp.sum((flat_e[:, None] == experts[None, :]).astype(jnp.int32), axis=0)
    padded = (counts + tm - 1) // tm * tm
    start = jnp.cumsum(counts) - counts
    pend = jnp.cumsum(padded)
    pstart = pend - padded
    e_sorted = flat_e[order]
    dest_sorted = pstart[e_sorted] + slot_i - start[e_sorted]
    slot_row = dest_sorted[jnp.argsort(order)].reshape(n_tok, TOP_K)
    blk_first = jnp.arange(n_blk, dtype=jnp.int32) * tm
    blk_exp = jnp.minimum(jnp.sum((pend[None, :] <= blk_first[:, None]).astype(jnp.int32), axis=1), N_EXPERTS - 1)
    n_used = pend[-1:] // tm
    row_e = jnp.repeat(blk_exp, tm)
    off = jnp.arange(n_blk * tm, dtype=jnp.int32) - pstart[row_e]
    live = jnp.logical_and(off < counts[row_e], jnp.repeat(blk_first < pend[-1], tm))
    slot_tok = jnp.where(live, order[jnp.clip(start[row_e] + off, 0, n_slot - 1)] // TOP_K, 0)
    return slot_tok.astype(jnp.int32), blk_exp.astype(jnp.int32), n_used.astype(jnp.int32), slot_row


def _moe_layer(layer, x, x_bf, n_tok, w_r, b_r, w_g, b_g, w_u, b_u, w_d, b_d, ln_g, ln_b, *, tm=TM_MOE):
    mp, d = x.shape
    w_pad = jnp.pad(w_r[layer], ((0, 0), (0, LANE - N_EXPERTS)))
    b_pad = jnp.pad(b_r[layer], (0, LANE - N_EXPERTS)).reshape(1, LANE)
    logits = _router(x_bf, w_pad, b_pad)[:n_tok, :N_EXPERTS]
    top_val, top_exp = lax.top_k(logits, TOP_K)
    gate = jax.nn.softmax(top_val, axis=-1)
    slot_tok, blk_exp, n_used, slot_row = _routing_tables(top_exp, n_tok, tm)
    y = _moe_ffn(x_bf[slot_tok], blk_exp, n_used, layer, w_g, b_g, w_u, b_u, w_d, b_d, tm=tm)
    rows = jnp.pad(slot_row, ((0, mp - n_tok), (0, 0))).T
    gate_p = jnp.pad(gate, ((0, mp - n_tok), (0, 0)))
    return _moe_combine_ln(x, y, rows, gate_p, ln_g[layer, 1], ln_b[layer, 1])


def _rope_tables(pos):
    half = HEAD_DIM // 2
    inv = ROPE_THETA ** (-2.0 * jnp.arange(half, dtype=F32) / HEAD_DIM)
    ang = pos.astype(F32)[:, None] * inv[None, :]
    cos, sin = jnp.cos(ang), jnp.sin(ang)
    return jnp.concatenate([cos, cos], axis=1), jnp.concatenate([-sin, sin], axis=1)


def _pad_rows(a, rows):
    return jnp.pad(a, ((0, rows - a.shape[0]),) + ((0, 0),) * (a.ndim - 1))


def _even_layer(i, x_bf, dims, tabs, cache_cmp, cache_sel, win_buf, pool_buf, page_table,
                w_in, w_c1, w_c2, pool_w, pool_scale, w_out):
    b, s, db, ds, past = dims
    mp = x_bf.shape[0]
    n_p = b * s
    dsp = 8
    proj = _mm(x_bf, w_in, layer=i, n_out=EVEN_MAIN, tn=TN_DENSE, rope_tabs=tabs)
    u = _mm(x_bf, w_in[i, :, EVEN_MAIN + EVEN_GATES:], n_out=POOL_DIM, tn=TN_DENSE)
    gates = _mm(x_bf, jnp.pad(w_in[i, :, EVEN_MAIN:EVEN_MAIN + EVEN_GATES], ((0, 0), (0, LANE - EVEN_GATES))),
                n_out=LANE, tn=LANE)
    kv_p = proj[:n_p, NSA_Q_DIM:].reshape(b, s, 3, 2 * NSA_KV_DIM)
    kv_s = proj[n_p:n_p + db * ds, NSA_Q_DIM:].reshape(db, ds, 3, 2 * NSA_KV_DIM)
    u_s = u[n_p:n_p + db * ds].reshape(db, ds, POOL_DIM)
    pw_bf = pool_w[i].astype(BF16)

    def seq_pad(a):
        return jnp.pad(a.reshape(db, ds, -1), ((0, 0), (0, dsp - ds), (0, 0))).reshape(db * dsp, -1)

    comp_p = _compress(kv_p[:, :, 0], w_c1[i], w_c2[i], nco=s // CMP_STRIDE)
    kv_cols = NSA_Q_DIM // HEAD_DIM
    o_p = _nsa_attend(proj, gates, comp_p, proj, proj, b=b, t=s, tq=TQ_NSA, pos_base=0,
                      n_cmp=s // CMP_STRIDE - 1, n_sel=s // SEL_BLOCK, nsk=s, nwt=s, nwk=WINDOW + TQ_NSA, kw_base=0,
                      win_slide=True, q_rows_per_batch=s, sel_cols=(kv_cols + 8, kv_cols + 12),
                      win_cols=(kv_cols + 16, kv_cols + 20))
    y_p = _pool_mix(u, jnp.zeros((b, POOL_HALO, POOL_DIM), F32), pw_bf, pool_scale[i], b=b, t=s, tq=TQ_POOL,
                    pos_base=0, rows_per_batch=s)

    n_pages = past // PAGE_SIZE
    pages = page_table + i * cache_cmp.shape[1]
    past_cmp = cache_cmp.reshape((-1,) + cache_cmp.shape[2:])[pages].reshape(db, past, 2 * NSA_KV_DIM)
    past_sel = cache_sel.reshape((-1,) + cache_sel.shape[2:])[pages].reshape(db, past, 2 * NSA_KV_DIM)
    n_blk_s = (past + ds + CMP_STRIDE - 1) // CMP_STRIDE - 1
    nch_s = -(-(n_blk_s + 1) // 8) * 8
    rows_c = jnp.concatenate([past_cmp, kv_s[:, :, 0]], axis=1)
    rows_c = jnp.pad(rows_c, ((0, 0), (0, nch_s * CMP_STRIDE - past - ds), (0, 0)))
    comp_s = _compress(rows_c, w_c1[i], w_c2[i], nco=-(-nch_s // LANE) * LANE)
    n_sel_s = -(-(past + ds) // SEL_BLOCK)
    nsk_s = -(-(past + ds) // LANE) * LANE
    sel_s = jnp.pad(jnp.concatenate([past_sel, kv_s[:, :, 1]], axis=1), ((0, 0), (0, nsk_s - past - ds), (0, 0)))
    wb = win_buf.shape[2]
    nwk_s = -(-(wb + ds) // LANE) * LANE
    win_s = jnp.concatenate([win_buf[i].reshape(db, wb, 2 * NSA_KV_DIM), kv_s[:, :, 2]], axis=1)
    win_state_s = win_s[:, -wb:]
    win_s = jnp.pad(win_s, ((0, 0), (0, nwk_s - wb - ds), (0, 0)))
    o_s = _nsa_attend(seq_pad(proj[n_p:n_p + db * ds, :NSA_Q_DIM]), seq_pad(gates[n_p:n_p + db * ds]), comp_s,
                      sel_s, win_s, b=db, t=dsp, tq=dsp, pos_base=past, n_cmp=n_blk_s, n_sel=n_sel_s, nsk=nsk_s,
                      nwt=nwk_s, nwk=nwk_s, kw_base=past - wb, win_slide=False, q_rows_per_batch=dsp,
                      sel_cols=(0, 4), win_cols=(0, 4))
    prev_s = jnp.pad(pool_buf[i], ((0, 0), (POOL_HALO - POOL_BUF, 0), (0, 0)))
    y_s = _pool_mix(seq_pad(u[n_p:n_p + db * ds]), prev_s, pw_bf, pool_scale[i], b=db, t=dsp, tq=dsp,
                    pos_base=past, rows_per_batch=dsp)

    def seq_unpad(a):
        return a.reshape(db, dsp, -1)[:, :ds].reshape(db * ds, -1)

    mix = jnp.concatenate([jnp.concatenate([o_p, y_p], axis=1),
                           jnp.concatenate([seq_unpad(o_s), seq_unpad(y_s)], axis=1)], axis=0)
    y = _mm(_pad_rows(mix, mp), w_out, layer=i, n_out=D_MODEL, tn=TN_DENSE)
    kv_shape = (2, NSA_KV_HEADS, HEAD_DIM)
    wbp = min(WINDOW, past)
    win_state_p = jnp.pad(kv_p[:, :, 2], ((0, 0), (max(0, wbp - s), 0), (0, 0)))[:, -wbp:]
    pool_state_s = jnp.concatenate([pool_buf[i], u_s], axis=1)[:, -POOL_BUF:]
    states = (kv_p[:, :, 0].reshape((b, s) + kv_shape), kv_s[:, :, 0].reshape((db, ds) + kv_shape),
              kv_p[:, :, 1].reshape((b, s) + kv_shape), kv_s[:, :, 1].reshape((db, ds) + kv_shape),
              win_state_p.reshape((b, wbp) + kv_shape), win_state_s.reshape((db, wb) + kv_shape),
              u[:n_p].reshape(b, s, POOL_DIM)[:, -POOL_BUF:], pool_state_s)
    return y, states


def _odd_layer(i, x_bf, dims, state_gla, w_in, w_gate2, b_gate, norm_g, w_out):
    b, s, db, ds, _ = dims
    mp = x_bf.shape[0]
    n_p = b * s
    proj = _mm(x_bf, w_in, layer=i, n_out=ODD_MAIN, tn=TN_DENSE)
    lr = _mm(x_bf, jnp.pad(w_in[i, :, ODD_MAIN:], ((0, 0), (0, LANE - GLA_GATE_RANK))), n_out=LANE, tn=LANE)
    wg2 = jnp.pad(w_gate2[i], ((0, LANE - GLA_GATE_RANK), (0, 0))).astype(BF16)
    o_p, g_p = _gla(proj, lr, wg2, b_gate[i], norm_g[i], jnp.zeros((b, GLA_HEADS, GLA_DK, GLA_DV), F32),
                    b=b, t=s, tq=TQ_GLA, t_valid=s, rows_per_batch=s)
    dsp = GLA_CHUNK

    def seq_pad(a):
        return jnp.pad(a.reshape(db, ds, -1), ((0, 0), (0, dsp - ds), (0, 0))).reshape(db * dsp, -1)

    o_s, g_s = _gla(seq_pad(proj[n_p:n_p + db * ds]), seq_pad(lr[n_p:n_p + db * ds]), wg2, b_gate[i], norm_g[i],
                    state_gla[i], b=db, t=dsp, tq=dsp, t_valid=ds, rows_per_batch=dsp)
    o_s = o_s.reshape(db, dsp, -1)[:, :ds].reshape(db * ds, -1)
    y = _mm(_pad_rows(jnp.concatenate([o_p, o_s], axis=0), mp), w_out, layer=i, n_out=D_MODEL, tn=TN_DENSE)
    return y, (g_p, g_s)


def kernel(x_prompt, x_sample, cache_nsa_cmp, cache_nsa_sel, state_nsa_win, state_pool, state_gla, page_table,
           nsa_w_in, nsa_w_cmp1, nsa_w_cmp2, pool_w, pool_scale, even_w_out,
           gla_w_in, gla_w_gate2, gla_b_gate, gla_norm_g, odd_w_out,
           moe_w_router, moe_b_router, moe_w_gate, moe_b_gate, moe_w_up, moe_b_up, moe_w_down, moe_b_down,
           ln_g, ln_b):
    b, s, d = x_prompt.shape
    db, ds, _ = x_sample.shape
    past = page_table.shape[1] * PAGE_SIZE
    dims = (b, s, db, ds, past)
    n_p = b * s
    n_tok = n_p + db * ds
    mp_align = int(np.lcm.reduce([TM_DENSE, TM_ROW, TM_COMB]))
    mp = -(-n_tok // mp_align) * mp_align
    x = _pad_rows(jnp.concatenate([x_prompt.reshape(n_p, d), x_sample.reshape(db * ds, d)], axis=0), mp)
    x_bf = x.astype(BF16)
    pos = jnp.concatenate([jnp.tile(jnp.arange(s, dtype=jnp.int32), b),
                           jnp.tile(past + jnp.arange(ds, dtype=jnp.int32), db),
                           jnp.zeros((mp - n_tok,), jnp.int32)])
    tabs = _rope_tables(pos)
    even_states, odd_states = [], []
    for layer in range(DEPTH):
        i = layer // 2
        if layer % 2 == 0:
            y, st = _even_layer(i, x_bf, dims, tabs, cache_nsa_cmp, cache_nsa_sel, state_nsa_win, state_pool,
                                page_table, nsa_w_in, nsa_w_cmp1, nsa_w_cmp2, pool_w, pool_scale, even_w_out)
            even_states.append(st)
        else:
            y, st = _odd_layer(i, x_bf, dims, state_gla, gla_w_in, gla_w_gate2, gla_b_gate, gla_norm_g, odd_w_out)
            odd_states.append(st)
        x, x_bf = _ln_res(x, y, ln_g[layer, 0], ln_b[layer, 0])
        x, x_bf = _moe_layer(layer, x, x_bf, n_tok, moe_w_router, moe_b_router, moe_w_gate, moe_b_gate,
                             moe_w_up, moe_b_up, moe_w_down, moe_b_down, ln_g, ln_b)
    outs = [x[:n_p].reshape(b, s, d), x[n_p:n_tok].reshape(db, ds, d)]
    for k in range(8):
        outs.append(jnp.stack([st[k] for st in even_states]))
    for k in range(2):
        outs.append(jnp.stack([st[k] for st in odd_states]))
    return tuple(outs)
```
